```python
import jax
import jax.numpy as jnp
from jax import lax
import numpy as np

D_MODEL = 1024
BATCH = 2
SEQ = 8192
DEPTH = 4
DEC_BATCH = 128
DEC_SEQ = 1
PAST_LEN = 2048
PAGE_SIZE = 128

N_REC_LAYERS = (DEPTH + 1) // 2
N_ATTN_LAYERS = DEPTH // 2
D_FF = 2816
FFN_RESIDUAL = 0.5
RMS_EPS = 1e-6
D_POOL = D_MODEL // 2
POOL_WINDOWS = (2, 4, 8, 16)
N_POOL_GROUPS = len(POOL_WINDOWS)
POOL_GROUP = D_POOL // N_POOL_GROUPS
POOL_HIST = max(POOL_WINDOWS) - 1
D_LRU = D_MODEL // 2
N_LRU_HEADS = 8
LRU_HEAD = D_LRU // N_LRU_HEADS
CONV_WIDTH = 4
LRU_C = 8.0
D_REC_IN = D_POOL + 2 * D_LRU
D_REC_OUT = D_POOL + D_LRU
N_HEADS = 16
HEAD_DIM = D_MODEL // N_HEADS
SB_BLOCK = 128
SB_BIAS_LO = -9.0
SB_BIAS_HI = -6.0

kernel_name = 'hybrid_pool_rglru_stickbreak_decoder_step'


def _rms(x, g):
    xf = x.astype(jnp.float32)
    y = xf * lax.rsqrt(jnp.mean(xf * xf, axis=-1, keepdims=True) + RMS_EPS)
    return (y * g.astype(jnp.float32)).astype(x.dtype)


def _swiglu(x, w_in, w_out):
    gate, up = jnp.split(x @ w_in, 2, axis=-1)
    return (jax.nn.silu(gate) * up) @ w_out


def _pool_mix(u, hist, w_grp, scale, start_pos):
    b, t = u.shape[0], u.shape[1]
    ext = jnp.concatenate([hist.astype(u.dtype), u], axis=1).astype(jnp.float32)
    cs = jnp.cumsum(jnp.pad(ext, ((0, 0), (1, 0), (0, 0))), axis=1)
    end = cs[:, POOL_HIST + 1:]
    pos = start_pos + jnp.arange(t)
    means = []
    for g, w in enumerate(POOL_WINDOWS):
        sl = slice(g * POOL_GROUP, (g + 1) * POOL_GROUP)
        start = cs[:, POOL_HIST + 1 - w: POOL_HIST + 1 - w + t, sl]
        cnt = jnp.minimum(w, pos + 1).astype(jnp.float32)[None, :, None]
        means.append((end[..., sl] - start) / cnt)
    pooled = (jnp.concatenate(means, axis=-1) - ext[:, POOL_HIST:]).astype(u.dtype)
    pooled = pooled.reshape(b, t, N_POOL_GROUPS, POOL_GROUP)
    mixed = jnp.einsum('btgc,gcd->btgd', pooled, w_grp).reshape(b, t, D_POOL)
    return mixed * scale, ext[:, -POOL_HIST:].astype(hist.dtype)


def _causal_conv(u, hist, w, bias):
    t = u.shape[1]
    ext = jnp.concatenate([hist.astype(u.dtype), u], axis=1)
    y = bias + ext[:, 0:t] * w[0]
    for j in range(1, CONV_WIDTH):
        y = y + ext[:, j:j + t] * w[j]
    return y, ext[:, -(CONV_WIDTH - 1):].astype(hist.dtype)


def _rg_lru(x, h0, w_r, b_r, w_i, b_i, lam):
    b, t = x.shape[0], x.shape[1]
    xh = x.reshape(b, t, N_LRU_HEADS, LRU_HEAD)
    r = jax.nn.sigmoid(jnp.einsum('bthi,hij->bthj', xh, w_r).reshape(b, t, D_LRU) + b_r)
    i = jax.nn.sigmoid(jnp.einsum('bthi,hij->bthj', xh, w_i).reshape(b, t, D_LRU) + b_i)
    log_a = LRU_C * r.astype(jnp.float32) * jax.nn.log_sigmoid(lam.astype(jnp.float32))
    a = jnp.exp(log_a)
    bx = jnp.sqrt(-jnp.expm1(2.0 * log_a)) * (i * x).astype(jnp.float32)

    def step(h, ab):
        h = ab[0] * h + ab[1]
        return h, h

    h_last, hs = lax.scan(step, h0.astype(jnp.float32), (a.swapaxes(0, 1), bx.swapaxes(0, 1)))
    return hs.swapaxes(0, 1).astype(x.dtype), h_last


def _rec_mixer(h, pool_hist, conv_hist, h0, start_pos, w_in, w_pool, scale, conv_w, conv_b,
               w_r, b_r, w_i, b_i, lam, w_out):
    proj = h @ w_in
    u_pool = proj[..., :D_POOL]
    u_x = proj[..., D_POOL:D_POOL + D_LRU]
    u_gate = proj[..., D_POOL + D_LRU:]
    pool_out, new_pool = _pool_mix(u_pool, pool_hist, w_pool, scale, start_pos)
    conv_out, new_conv = _causal_conv(u_x, conv_hist, conv_w, conv_b)
    lru_out, h_last = _rg_lru(conv_out, h0, w_r, b_r, w_i, b_i, lam)
    lru_out = lru_out * jax.nn.gelu(u_gate)
    out = jnp.concatenate([pool_out.astype(h.dtype), lru_out], axis=-1) @ w_out
    return out, new_pool, new_conv, h_last


def _qkv_heads(h, w_qkv):
    b, t = h.shape[0], h.shape[1]
    q, k, v = jnp.split(h @ w_qkv, 3, axis=-1)
    shp = (b, t, N_HEADS, HEAD_DIM)
    return q.reshape(shp), k.reshape(shp), v.reshape(shp)


def _sb_attend(q, k, v, bias, q_pos, k_pos):
    z = jnp.einsum('bqhd,bkhd->bhqk', q.astype(jnp.float32), k.astype(jnp.float32)) * (HEAD_DIM ** -0.5)
    z = z + bias.astype(jnp.float32)[None, :, None, None]
    mask = k_pos[None, :] < q_pos[:, None]
    sp = jnp.where(mask, jax.nn.softplus(z), 0.0)
    after = lax.cumsum(sp, axis=3, reverse=True) - sp
    w = jnp.where(mask, jnp.exp(jax.nn.log_sigmoid(z) - after), 0.0)
    return jnp.einsum('bhqk,bkhd->bqhd', w, v.astype(jnp.float32)).astype(v.dtype)


def _sb_blocked(q, k, v, bias, q_pos, k_pos):
    b, t, h, d = q.shape
    if t <= SB_BLOCK or t % SB_BLOCK != 0:
        return _sb_attend(q, k, v, bias, q_pos, k_pos)
    nb = t // SB_BLOCK
    qb = q.reshape(b, nb, SB_BLOCK, h, d).swapaxes(0, 1)
    pb = q_pos.reshape(nb, SB_BLOCK)
    out = lax.map(lambda qp: _sb_attend(qp[0], k, v, bias, qp[1], k_pos), (qb, pb))
    return out.swapaxes(0, 1).reshape(b, t, h, d)


def setup_inputs(seed: int = 0) -> dict:
    key = jax.random.key(seed)
    ks = iter(jax.random.split(key, 40))

    def nrm(shape, s):
        return jax.random.normal(next(ks), shape, jnp.float32) * s

    n_pages = PAST_LEN // PAGE_SIZE
    n_used = DEC_BATCH * n_pages
    n_phys = n_used + max(1, n_used // 4)
    x_prompt = nrm((BATCH, SEQ, D_MODEL), 1.0)
    x_sample = nrm((DEC_BATCH, DEC_SEQ, D_MODEL), 1.0)
    cache_k = nrm((N_ATTN_LAYERS, n_phys, PAGE_SIZE, N_HEADS, HEAD_DIM), 1.0)
    cache_v = nrm((N_ATTN_LAYERS, n_phys, PAGE_SIZE, N_HEADS, HEAD_DIM), 1.0)
    page_table = jax.random.permutation(next(ks), n_phys)[:n_used].reshape(DEC_BATCH, n_pages).astype(jnp.int32)
    state_pool = nrm((N_REC_LAYERS, DEC_BATCH, POOL_HIST, D_POOL), 1.0)
    state_conv = nrm((N_REC_LAYERS, DEC_BATCH, CONV_WIDTH - 1, D_LRU), 1.0)
    state_h = nrm((N_REC_LAYERS, DEC_BATCH, D_LRU), 0.5)
    norm_ffn_a = 1.0 + nrm((DEPTH, D_MODEL), 0.05)
    ffn_a_in = nrm((DEPTH, D_MODEL, 2 * D_FF), D_MODEL ** -0.5)
    ffn_a_out = nrm((DEPTH, D_FF, D_MODEL), D_FF ** -0.5)
    norm_mix = 1.0 + nrm((DEPTH, D_MODEL), 0.05)
    norm_ffn_b = 1.0 + nrm((DEPTH, D_MODEL), 0.05)
    ffn_b_in = nrm((DEPTH, D_MODEL, 2 * D_FF), D_MODEL ** -0.5)
    ffn_b_out = nrm((DEPTH, D_FF, D_MODEL), D_FF ** -0.5)
    rec_w_in = nrm((N_REC_LAYERS, D_MODEL, D_REC_IN), D_MODEL ** -0.5)
    pool_w = nrm((N_REC_LAYERS, N_POOL_GROUPS, POOL_GROUP, POOL_GROUP), POOL_GROUP ** -0.5)
    pool_scale = 1.0 + nrm((N_REC_LAYERS, D_POOL), 0.1)
    conv_w = nrm((N_REC_LAYERS, CONV_WIDTH, D_LRU), CONV_WIDTH ** -0.5)
    conv_b = nrm((N_REC_LAYERS, D_LRU), 0.02)
    gate_r_w = nrm((N_REC_LAYERS, N_LRU_HEADS, LRU_HEAD, LRU_HEAD), LRU_HEAD ** -0.5)
    gate_r_b = nrm((N_REC_LAYERS, D_LRU), 0.1)
    gate_i_w = nrm((N_REC_LAYERS, N_LRU_HEADS, LRU_HEAD, LRU_HEAD), LRU_HEAD ** -0.5)
    gate_i_b = nrm((N_REC_LAYERS, D_LRU), 0.1)
    a_c = jax.random.uniform(next(ks), (N_REC_LAYERS, D_LRU), jnp.float32, minval=0.9, maxval=0.999)
    a_base = a_c ** (1.0 / LRU_C)
    lru_lambda = jnp.log(a_base) - jnp.log1p(-a_base)
    rec_w_out = nrm((N_REC_LAYERS, D_REC_OUT, D_MODEL), D_REC_OUT ** -0.5)
    attn_w_qkv = nrm((N_ATTN_LAYERS, D_MODEL, 3 * D_MODEL), D_MODEL ** -0.5)
    attn_logit_bias = jax.random.uniform(next(ks), (N_ATTN_LAYERS, N_HEADS), jnp.float32,
                                         minval=SB_BIAS_LO, maxval=SB_BIAS_HI)
    attn_w_out = nrm((N_ATTN_LAYERS, D_MODEL, D_MODEL), D_MODEL ** -0.5)
    norm_final = 1.0 + nrm((D_MODEL,), 0.05)
    return {'x_prompt': x_prompt, 'x_sample': x_sample, 'cache_k': cache_k, 'cache_v': cache_v,
            'page_table': page_table, 'state_pool': state_pool, 'state_conv': state_conv, 'state_h': state_h,
            'norm_ffn_a': norm_ffn_a, 'ffn_a_in': ffn_a_in, 'ffn_a_out': ffn_a_out, 'norm_mix': norm_mix,
            'norm_ffn_b': norm_ffn_b, 'ffn_b_in': ffn_b_in, 'ffn_b_out': ffn_b_out, 'rec_w_in': rec_w_in,
            'pool_w': pool_w, 'pool_scale': pool_scale, 'conv_w': conv_w, 'conv_b': conv_b,
            'gate_r_w': gate_r_w, 'gate_r_b': gate_r_b, 'gate_i_w': gate_i_w, 'gate_i_b': gate_i_b,
            'lru_lambda': lru_lambda, 'rec_w_out': rec_w_out, 'attn_w_qkv': attn_w_qkv,
            'attn_logit_bias': attn_logit_bias, 'attn_w_out': attn_w_out, 'norm_final': norm_final}


def reference(x_prompt, x_sample, cache_k, cache_v, page_table, state_pool, state_conv, state_h,
              norm_ffn_a, ffn_a_in, ffn_a_out, norm_mix, norm_ffn_b, ffn_b_in, ffn_b_out,
              rec_w_in, pool_w, pool_scale, conv_w, conv_b, gate_r_w, gate_r_b, gate_i_w, gate_i_b,
              lru_lambda, rec_w_out, attn_w_qkv, attn_logit_bias, attn_w_out, norm_final):
    past_len = page_table.shape[1] * cache_k.shape[2]
    n_prompt, t_prompt = x_prompt.shape[0], x_prompt.shape[1]
    n_dec, t_dec = x_sample.shape[0], x_sample.shape[1]
    xp, xs = x_prompt, x_sample
    k_p, v_p, k_s, v_s = [], [], [], []
    pool_p, conv_p, h_p, pool_s, conv_s, h_s = [], [], [], [], [], []
    for l in range(DEPTH):
        xp = xp + FFN_RESIDUAL * _swiglu(_rms(xp, norm_ffn_a[l]), ffn_a_in[l], ffn_a_out[l])
        xs = xs + FFN_RESIDUAL * _swiglu(_rms(xs, norm_ffn_a[l]), ffn_a_in[l], ffn_a_out[l])
        hp = _rms(xp, norm_mix[l])
        hs = _rms(xs, norm_mix[l])
        if l % 2 == 0:
            r = l // 2
            params = (rec_w_in[r], pool_w[r], pool_scale[r], conv_w[r], conv_b[r], gate_r_w[r], gate_r_b[r],
                      gate_i_w[r], gate_i_b[r], lru_lambda[r], rec_w_out[r])
            zero_pool = jnp.zeros((n_prompt, POOL_HIST, D_POOL), state_pool.dtype)
            zero_conv = jnp.zeros((n_prompt, CONV_WIDTH - 1, D_LRU), state_conv.dtype)
            zero_h = jnp.zeros((n_prompt, D_LRU), jnp.float32)
            mp, np_pool, np_conv, np_h = _rec_mixer(hp, zero_pool, zero_conv, zero_h, 0, *params)
            ms, ns_pool, ns_conv, ns_h = _rec_mixer(hs, state_pool[r], state_conv[r], state_h[r], past_len, *params)
            pool_p.append(np_pool)
            conv_p.append(np_conv)
            h_p.append(np_h.astype(state_h.dtype))
            pool_s.append(ns_pool)
            conv_s.append(ns_conv)
            h_s.append(ns_h.astype(state_h.dtype))
        else:
            a = l // 2
            bias = attn_logit_bias[a]
            qp, kp, vp = _qkv_heads(hp, attn_w_qkv[a])
            pos_p = jnp.arange(t_prompt)
            op = _sb_blocked(qp, kp, vp, bias, pos_p, pos_p)
            qs, ks_new, vs_new = _qkv_heads(hs, attn_w_qkv[a])
            k_past = cache_k[a][page_table].reshape(n_dec, past_len, N_HEADS, HEAD_DIM)
            v_past = cache_v[a][page_table].reshape(n_dec, past_len, N_HEADS, HEAD_DIM)
            k_all = jnp.concatenate([k_past.astype(ks_new.dtype), ks_new], axis=1)
            v_all = jnp.concatenate([v_past.astype(vs_new.dtype), vs_new], axis=1)
            o_s = _sb_blocked(qs, k_all, v_all, bias, past_len + jnp.arange(t_dec), jnp.arange(past_len + t_dec))
            mp = op.reshape(n_prompt, t_prompt, D_MODEL) @ attn_w_out[a]
            ms = o_s.reshape(n_dec, t_dec, D_MODEL) @ attn_w_out[a]
            k_p.append(kp.astype(cache_k.dtype))
            v_p.append(vp.astype(cache_v.dtype))
            k_s.append(ks_new.astype(cache_k.dtype))
            v_s.append(vs_new.astype(cache_v.dtype))
        xp = xp + mp
        xs = xs + ms
        xp = xp + FFN_RESIDUAL * _swiglu(_rms(xp, norm_ffn_b[l]), ffn_b_in[l], ffn_b_out[l])
        xs = xs + FFN_RESIDUAL * _swiglu(_rms(xs, norm_ffn_b[l]), ffn_b_in[l], ffn_b_out[l])
    y_prompt = _rms(xp, norm_final)
    y_sample = _rms(xs, norm_final)
    return (y_prompt, y_sample, jnp.stack(k_p), jnp.stack(v_p), jnp.stack(k_s), jnp.stack(v_s),
            jnp.stack(pool_p), jnp.stack(conv_p), jnp.stack(h_p), jnp.stack(pool_s), jnp.stack(conv_s),
            jnp.stack(h_s))
```

```python
import functools

import jax
import jax.numpy as jnp
from jax import lax
from jax.experimental import pallas as pl
from jax.experimental.pallas import tpu as pltpu

F32 = jnp.float32
BF16 = jnp.bfloat16

RMS_EPS = 1e-6
FFN_RESIDUAL = 0.5
POOL_WINDOWS = (2, 4, 8, 16)
POOL_HIST = max(POOL_WINDOWS) - 1
CONV_WIDTH = 4
LRU_C = 8.0
HEAD_DIM = 64

V7X_VMEM_BYTES = 64 * 2**20
V7X_MXU_DIM = 256
LANES = 128
SUBLANES = 8
VMEM_LIMIT = V7X_VMEM_BYTES - 8 * 2**20

TOKEN_TILE_CAP = 512
REC_TILE = 256
ATTN_TILE = V7X_MXU_DIM
FFN_CHUNK = 1024
POOL_CARRY = 16
CONV_CARRY = 8


def _pick_tile(n, cap):
    best = SUBLANES
    for t in range(SUBLANES, cap + 1, SUBLANES):
        if n % t == 0:
            best = t
    return best


def _params(*sem):
    return pltpu.CompilerParams(dimension_semantics=sem, vmem_limit_bytes=VMEM_LIMIT)


def _const_spec(shape):
    nd = len(shape)
    return pl.BlockSpec(shape, lambda *_: (0,) * nd, pipeline_mode=pl.Buffered(1))


def _rms(x, g):
    ms = jnp.mean(x * x, axis=-1, keepdims=True)
    return x * lax.rsqrt(ms + RMS_EPS) * g


def _dot(a, b):
    return jnp.dot(a, b, preferred_element_type=F32)


def _softplus(z):
    return jnp.maximum(z, 0.0) + jnp.log(1.0 + jnp.exp(-jnp.abs(z)))


def _blockdiag(w, grp):
    h, c, d = w.shape
    w = w.reshape(h // grp, grp, c, d)
    eye = jnp.eye(grp, dtype=w.dtype)
    return jnp.einsum('kgcd,gh->kgchd', w, eye).reshape(h // grp, grp * c, grp * d)


def _ffn_body(*refs, chunks, has_pre, has_final):
    it = iter(refs)
    x_ref = next(it)
    if has_pre:
        o_ref, wo_ref = next(it), next(it)
    g_ref, win_ref, wout_ref = next(it), next(it), next(it)
    if has_final:
        gf_ref = next(it)
    out_ref = next(it)

    x = x_ref[...]
    if has_pre:
        x = x + _dot(o_ref[...], wo_ref[...])
    h = _rms(x, g_ref[...]).astype(BF16)
    dff = wout_ref.shape[0]
    acc = None
    for s, n in chunks:
        gate = _dot(h, win_ref[:, s:s + n])
        up = _dot(h, win_ref[:, dff + s:dff + s + n])
        act = (gate * jax.nn.sigmoid(gate) * up).astype(BF16)
        part = _dot(act, wout_ref[s:s + n, :])
        acc = part if acc is None else acc + part
    y = x + FFN_RESIDUAL * acc
    if has_final:
        y = _rms(y, gf_ref[...])
    out_ref[...] = y


def _ffn(x, g, w_in, w_out, pre=None, final_g=None):
    n, d = x.shape
    dff = w_out.shape[0]
    tm = _pick_tile(n, TOKEN_TILE_CAP)
    chunks = tuple((s, min(FFN_CHUNK, dff - s)) for s in range(0, dff, FFN_CHUNK))
    row = lambda width: pl.BlockSpec((tm, width), lambda i: (i, 0))
    args, specs = [x], [row(d)]
    if pre is not None:
        o, wo = pre
        args += [o, wo]
        specs += [row(o.shape[1]), _const_spec(wo.shape)]
    args += [g.reshape(1, d), w_in, w_out]
    specs += [_const_spec((1, d)), _const_spec(w_in.shape), _const_spec(w_out.shape)]
    if final_g is not None:
        args.append(final_g.reshape(1, d))
        specs.append(_const_spec((1, d)))
    body = functools.partial(_ffn_body, chunks=chunks, has_pre=pre is not None, has_final=final_g is not None)
    return pl.pallas_call(
        body,
        grid=(n // tm,),
        in_specs=specs,
        out_specs=row(d),
        out_shape=jax.ShapeDtypeStruct((n, d), F32),
        input_output_aliases={0: 0},
        compiler_params=_params("arbitrary"),
        name="ffn",
    )(*args)


def _proj_body(x_ref, g_ref, w_ref, *out_refs):
    h = _rms(x_ref[...], g_ref[...]).astype(BF16)
    wd = w_ref.shape[1] // len(out_refs)
    for k, o in enumerate(out_refs):
        o[...] = _dot(h, w_ref[:, k * wd:(k + 1) * wd])


def _proj(x, g, w, n_out):
    n, d = x.shape
    wd = w.shape[1] // n_out
    tm = _pick_tile(n, TOKEN_TILE_CAP)
    return pl.pallas_call(
        _proj_body,
        grid=(n // tm,),
        in_specs=[pl.BlockSpec((tm, d), lambda i: (i, 0)), _const_spec((1, d)), _const_spec(w.shape)],
        out_specs=[pl.BlockSpec((tm, wd), lambda i: (i, 0))] * n_out,
        out_shape=[jax.ShapeDtypeStruct((n, wd), F32)] * n_out,
        compiler_params=_params("arbitrary"),
        name="proj",
    )(x, g.reshape(1, d), w)


def _grouped_dot(x, w_ref):
    blk = w_ref.shape[1]
    xb = x.astype(BF16)
    return jnp.concatenate([_dot(xb[:, k * blk:(k + 1) * blk], w_ref[k]) for k in range(w_ref.shape[0])], axis=-1)


def _log_sigmoid(x):
    return jnp.minimum(x, 0.0) - jnp.log(1.0 + jnp.exp(-jnp.abs(x)))


def _lru_coeffs(conv, wr_ref, br_ref, wi_ref, bi_ref, lam_ref):
    r = jax.nn.sigmoid(_grouped_dot(conv, wr_ref) + br_ref[...])
    i = jax.nn.sigmoid(_grouped_dot(conv, wi_ref) + bi_ref[...])
    log_a = LRU_C * r * _log_sigmoid(lam_ref[...])
    a = jnp.exp(log_a)
    b = jnp.sqrt(1.0 - jnp.exp(2.0 * log_a)) * (i * conv)
    return a, b


def _rec_prompt_body(up_ref, ux_ref, ug_ref, wp_ref, ps_ref, cw_ref, cb_ref, wr_ref, br_ref, wi_ref, bi_ref,
                     lam_ref, mix_ref, pst_ref, cst_ref, hst_ref, pext, cext, a_scr, b_scr, hs_scr, h_scr, *, tm):
    i = pl.program_id(1)
    dp = up_ref.shape[1]
    pg = dp // len(POOL_WINDOWS)

    @pl.when(i == 0)
    def _():
        pext[0:POOL_CARRY, :] = jnp.zeros((POOL_CARRY, dp), F32)
        cext[0:CONV_CARRY, :] = jnp.zeros((CONV_CARRY, cext.shape[1]), F32)
        h_scr[...] = jnp.zeros(h_scr.shape, F32)

    up = up_ref[...]
    pext[POOL_CARRY:POOL_CARRY + tm, :] = up
    pos = i * tm + lax.broadcasted_iota(jnp.int32, (tm, 1), 0)
    pooled = []
    for g, w in enumerate(POOL_WINDOWS):
        cols = slice(g * pg, (g + 1) * pg)
        s = up[:, cols]
        for j in range(1, w):
            s = s + pext[POOL_CARRY - j:POOL_CARRY - j + tm, cols]
        cnt = jnp.minimum(w, pos + 1).astype(F32)
        pooled.append(s / cnt - up[:, cols])
    pool_out = _grouped_dot(jnp.concatenate(pooled, axis=-1), wp_ref) * ps_ref[...]

    ux = ux_ref[...]
    cext[CONV_CARRY:CONV_CARRY + tm, :] = ux
    conv = cb_ref[...] + ux * cw_ref[CONV_WIDTH - 1:CONV_WIDTH, :]
    for j in range(CONV_WIDTH - 1):
        off = CONV_CARRY - (CONV_WIDTH - 1) + j
        conv = conv + cext[off:off + tm, :] * cw_ref[j:j + 1, :]

    a, b = _lru_coeffs(conv, wr_ref, br_ref, wi_ref, bi_ref, lam_ref)
    a_scr[...] = a
    b_scr[...] = b

    def step(t, h):
        h = a_scr[pl.ds(t, 1), :] * h + b_scr[pl.ds(t, 1), :]
        hs_scr[pl.ds(t, 1), :] = h
        return h

    h_last = lax.fori_loop(0, tm, step, h_scr[0:1, :], unroll=8)
    h_scr[...] = jnp.broadcast_to(h_last, h_scr.shape)

    lru_out = hs_scr[...] * jax.nn.gelu(ug_ref[...])
    mix_ref[:, 0:dp] = pool_out.astype(mix_ref.dtype)
    mix_ref[:, dp:] = lru_out.astype(mix_ref.dtype)

    pext[0:POOL_CARRY, :] = pext[tm:tm + POOL_CARRY, :]
    cext[0:CONV_CARRY, :] = cext[tm:tm + CONV_CARRY, :]
    pst_ref[...] = pext[0:POOL_CARRY, :]
    cst_ref[...] = cext[0:CONV_CARRY, :]
    hst_ref[...] = h_scr[...]


def _rec_weights(pool_w, pool_scale, conv_w, conv_b, w_r, b_r, w_i, b_i, lam):
    dp = pool_scale.shape[0]
    dl = conv_b.shape[0]
    wp = _blockdiag(pool_w, V7X_MXU_DIM // pool_w.shape[1]).astype(BF16)
    wr = _blockdiag(w_r, V7X_MXU_DIM // w_r.shape[1]).astype(BF16)
    wi = _blockdiag(w_i, V7X_MXU_DIM // w_i.shape[1]).astype(BF16)
    return (wp, pool_scale.reshape(1, dp), conv_w, conv_b.reshape(1, dl), wr, b_r.reshape(1, dl), wi,
            b_i.reshape(1, dl), lam.reshape(1, dl))


def _rec_prompt(u_pool, u_x, u_gate, weights, nb, t):
    dp, dl = u_pool.shape[1], u_x.shape[1]
    tm = _pick_tile(t, REC_TILE)
    nt = t // tm
    row = lambda width: pl.BlockSpec((tm, width), lambda b, i: (b * nt + i, 0))
    state = lambda rows, width: pl.BlockSpec((None, rows, width), lambda b, i: (b, 0, 0))
    return pl.pallas_call(
        functools.partial(_rec_prompt_body, tm=tm),
        grid=(nb, nt),
        in_specs=[row(dp), row(dl), row(dl)] + [_const_spec(w.shape) for w in weights],
        out_specs=[row(dp + dl), state(POOL_CARRY, dp), state(CONV_CARRY, dl), state(SUBLANES, dl)],
        out_shape=[jax.ShapeDtypeStruct((nb * t, dp + dl), BF16),
                   jax.ShapeDtypeStruct((nb, POOL_CARRY, dp), F32),
                   jax.ShapeDtypeStruct((nb, CONV_CARRY, dl), F32),
                   jax.ShapeDtypeStruct((nb, SUBLANES, dl), F32)],
        scratch_shapes=[pltpu.VMEM((tm + POOL_CARRY, dp), F32), pltpu.VMEM((tm + CONV_CARRY, dl), F32),
                        pltpu.VMEM((tm, dl), F32), pltpu.VMEM((tm, dl), F32), pltpu.VMEM((tm, dl), F32),
                        pltpu.VMEM((SUBLANES, dl), F32)],
        compiler_params=_params("arbitrary", "arbitrary"),
        name="rec_prompt",
    )(u_pool, u_x, u_gate, *weights)


def _rec_sample_body(up_ref, ux_ref, ug_ref, sp_ref, sc_ref, sh_ref, wp_ref, ps_ref, cw_ref, cb_ref, wr_ref,
                     br_ref, wi_ref, bi_ref, lam_ref, mix_ref, np_ref, nc_ref, nh_ref, *, start_pos):
    dp = up_ref.shape[1]
    dl = ux_ref.shape[1]
    pg = dp // len(POOL_WINDOWS)
    up = up_ref[...]
    pooled = []
    for g, w in enumerate(POOL_WINDOWS):
        s = up[:, g * pg:(g + 1) * pg]
        for j in range(1, w):
            base = (POOL_HIST - j) * dp + g * pg
            s = s + sp_ref[:, base:base + pg]
        pooled.append(s / float(min(w, start_pos + 1)) - up[:, g * pg:(g + 1) * pg])
    pool_out = _grouped_dot(jnp.concatenate(pooled, axis=-1), wp_ref) * ps_ref[...]
    np_ref[:, 0:(POOL_HIST - 1) * dp] = sp_ref[:, dp:]
    np_ref[:, (POOL_HIST - 1) * dp:] = up

    ux = ux_ref[...]
    conv = cb_ref[...] + ux * cw_ref[CONV_WIDTH - 1:CONV_WIDTH, :]
    for j in range(CONV_WIDTH - 1):
        conv = conv + sc_ref[:, j * dl:(j + 1) * dl] * cw_ref[j:j + 1, :]
    nc_ref[:, 0:(CONV_WIDTH - 2) * dl] = sc_ref[:, dl:]
    nc_ref[:, (CONV_WIDTH - 2) * dl:] = ux

    a, b = _lru_coeffs(conv, wr_ref, br_ref, wi_ref, bi_ref, lam_ref)
    h = a * sh_ref[...] + b
    nh_ref[...] = h
    mix_ref[:, 0:dp] = pool_out.astype(mix_ref.dtype)
    mix_ref[:, dp:] = (h * jax.nn.gelu(ug_ref[...])).astype(mix_ref.dtype)


def _rec_sample(u_pool, u_x, u_gate, s_pool, s_conv, s_h, weights, row0, start_pos):
    ns = s_h.shape[0]
    dp, dl = u_pool.shape[1], u_x.shape[1]
    blk = row0 // ns
    rows = lambda width: pl.BlockSpec((ns, width), lambda i: (blk, 0))
    full = lambda a: pl.BlockSpec(a.shape, lambda i: (0,) * a.ndim)
    outs = [jax.ShapeDtypeStruct((ns, dp + dl), BF16), jax.ShapeDtypeStruct(s_pool.shape, F32),
            jax.ShapeDtypeStruct(s_conv.shape, F32), jax.ShapeDtypeStruct(s_h.shape, F32)]
    return pl.pallas_call(
        functools.partial(_rec_sample_body, start_pos=start_pos),
        grid=(1,),
        in_specs=[rows(dp), rows(dl), rows(dl), full(s_pool), full(s_conv), full(s_h)] + [full(w) for w in weights],
        out_specs=[pl.BlockSpec(o.shape, lambda i: (0, 0)) for o in outs],
        out_shape=outs,
        compiler_params=_params("arbitrary"),
        name="rec_sample",
    )(u_pool, u_x, u_gate, s_pool, s_conv, s_h, *weights)


def _attn_prompt_body(bias_ref, q_ref, k_ref, v_ref, tri_ref, o_ref, kb_scr, vb_scr, *, tile):
    hp = pl.program_id(1)
    qi = pl.program_id(2)

    @pl.when(qi == 0)
    def _():
        kb_scr[...] = k_ref[...].astype(BF16)
        vb_scr[...] = v_ref[...].astype(BF16)

    lane = lax.broadcasted_iota(jnp.int32, (1, 2 * HEAD_DIM), 1)
    q = q_ref[...] * (HEAD_DIM ** -0.5)
    qh = [jnp.where((lane // HEAD_DIM) == h, q, 0.0).astype(BF16) for h in range(2)]
    bias = [bias_ref[2 * hp + h] for h in range(2)]
    tri = tri_ref[...]
    rows = lax.broadcasted_iota(jnp.int32, (tile, tile), 0)
    cols = lax.broadcasted_iota(jnp.int32, (tile, tile), 1)
    causal = cols < rows

    def block(j, state, masked):
        start = pl.multiple_of(j * tile, tile)
        kblk = kb_scr[pl.ds(start, tile), :]
        vblk = vb_scr[pl.ds(start, tile), :]
        out = []
        for h in range(2):
            acc, carry = state[h]
            z = lax.dot_general(qh[h], kblk, (((1,), (1,)), ((), ())), preferred_element_type=F32) + bias[h]
            sp = _softplus(z)
            if masked:
                sp = jnp.where(causal, sp, 0.0)
            local = _dot(sp.astype(BF16), tri)
            w = jnp.exp(z - (local + carry))
            if masked:
                w = jnp.where(causal, w, 0.0)
            acc = acc + _dot(w.astype(BF16), vblk)
            out.append((acc, carry + local[:, 0:1]))
        return tuple(out)

    zero = (jnp.zeros((tile, 2 * HEAD_DIM), F32), jnp.zeros((tile, 1), F32))
    state = block(qi, (zero, zero), True)
    state = lax.fori_loop(0, qi, lambda it, st: block(qi - 1 - it, st, False), state)
    o_ref[...] = jnp.where(lane < HEAD_DIM, state[0][0], state[1][0]).astype(o_ref.dtype)


def _attn_prompt(q, k, v, bias, nb, t):
    d = q.shape[1]
    tile = ATTN_TILE
    nq = t // tile
    width = 2 * HEAD_DIM
    tri = jnp.tril(jnp.ones((tile, tile), BF16))
    return pl.pallas_call(
        functools.partial(_attn_prompt_body, tile=tile),
        grid=(nb, d // width, nq),
        in_specs=[pl.BlockSpec(memory_space=pltpu.SMEM),
                  pl.BlockSpec((tile, width), lambda b, hp, qi: (b * nq + qi, hp)),
                  pl.BlockSpec((t, width), lambda b, hp, qi: (b, hp)),
                  pl.BlockSpec((t, width), lambda b, hp, qi: (b, hp)),
                  _const_spec((tile, tile))],
        out_specs=pl.BlockSpec((tile, width), lambda b, hp, qi: (b * nq + qi, hp)),
        out_shape=jax.ShapeDtypeStruct((nb * t, d), BF16),
        scratch_shapes=[pltpu.VMEM((t, width), BF16), pltpu.VMEM((t, width), BF16)],
        compiler_params=_params("arbitrary", "arbitrary", "arbitrary"),
        name="attn_prompt",
    )(bias, q, k, v, tri)


def _attn_decode_body(pt_ref, q_ref, k_ref, v_ref, bias_ref, ind_ref, indt_ref, tri_ref, o_ref, acc_scr, carry_scr):
    n = pl.program_id(0)
    p = pl.program_id(1)
    npages = pl.num_programs(1)

    @pl.when(p == 0)
    def _():
        acc_scr[...] = jnp.zeros(acc_scr.shape, F32)
        carry_scr[...] = jnp.zeros(carry_scr.shape, F32)

    q = q_ref[pl.ds(n, 1), :] * (HEAD_DIM ** -0.5)
    z = _dot((k_ref[...] * q).astype(BF16), ind_ref[...]) + bias_ref[...]
    sp = _softplus(z)
    local = _dot(tri_ref[...], sp.astype(BF16))
    w = jnp.exp(z - (local + carry_scr[0:1, :]))
    carry_scr[...] = jnp.broadcast_to(carry_scr[0:1, :] + local[0:1, :], carry_scr.shape)
    contrib = _dot(w.astype(BF16), indt_ref[...]) * v_ref[...]
    page = contrib.shape[0]
    acc_scr[...] += contrib.reshape(page // SUBLANES, SUBLANES, contrib.shape[1]).sum(axis=0)

    @pl.when(p == npages - 1)
    def _():
        o_ref[pl.ds(n, 1), :] = jnp.sum(acc_scr[...], axis=0, keepdims=True).astype(o_ref.dtype)


def _attn_decode(q, cache_k, cache_v, layer, page_table, bias, row0):
    ns, npages = page_table.shape
    _, nphys, page, nh, hd = cache_k.shape
    d = nh * hd
    ck = cache_k.reshape(cache_k.shape[0], nphys, page, d)
    cv = cache_v.reshape(cache_v.shape[0], nphys, page, d)
    head_of = jnp.arange(d) // hd
    ind = (head_of[:, None] == jnp.arange(LANES)[None, :]).astype(BF16)
    bias_row = jnp.zeros((1, LANES), F32).at[0, :nh].set(bias)
    tri = jnp.triu(jnp.ones((page, page), BF16))
    blk = row0 // ns
    page_map = lambda n, p, pt: (layer, pt[n * npages + npages - 1 - p], 0, 0)
    const = lambda shape: pl.BlockSpec(shape, lambda n, p, pt: (0,) * len(shape))
    grid_spec = pltpu.PrefetchScalarGridSpec(
        num_scalar_prefetch=1,
        grid=(ns, npages),
        in_specs=[pl.BlockSpec((ns, d), lambda n, p, pt: (blk, 0)),
                  pl.BlockSpec((None, None, page, d), page_map),
                  pl.BlockSpec((None, None, page, d), page_map),
                  const((1, LANES)), const((d, LANES)), const((LANES, d)), const((page, page))],
        out_specs=pl.BlockSpec((ns, d), lambda n, p, pt: (0, 0)),
        scratch_shapes=[pltpu.VMEM((SUBLANES, d), F32), pltpu.VMEM((SUBLANES, LANES), F32)],
    )
    return pl.pallas_call(
        _attn_decode_body,
        grid_spec=grid_spec,
        out_shape=jax.ShapeDtypeStruct((ns, d), F32),
        compiler_params=_params("arbitrary", "arbitrary"),
        name="attn_decode",
    )(page_table.reshape(-1), q, ck, cv, bias_row, ind, ind.T, tri).astype(BF16)


def kernel(x_prompt, x_sample, cache_k, cache_v, page_table, state_pool, state_conv, state_h, norm_ffn_a, ffn_a_in, ffn_a_out, norm_mix, norm_ffn_b, ffn_b_in, ffn_b_out, rec_w_in, pool_w, pool_scale, conv_w, conv_b, gate_r_w, gate_r_b, gate_i_w, gate_i_b, lru_lambda, rec_w_out, attn_w_qkv, attn_logit_bias, attn_w_out, norm_final):
    nb, t, d = x_prompt.shape
    ns, t_dec, _ = x_sample.shape
    assert t_dec == 1 and (nb * t) % ns == 0 and t % ATTN_TILE == 0
    depth = norm_mix.shape[0]
    nh = d // HEAD_DIM
    past_len = page_table.shape[1] * cache_k.shape[2]
    n_p = nb * t
    dp, dl = pool_scale.shape[1], conv_b.shape[1]

    x = jnp.concatenate([x_prompt.reshape(n_p, d), x_sample.reshape(ns, d)], axis=0)
    bf = lambda w: w.astype(BF16)

    k_p, v_p, k_s, v_s = [], [], [], []
    pool_p, conv_p, h_p, pool_s, conv_s, h_s = [], [], [], [], [], []
    pre = None
    for l in range(depth):
        x = _ffn(x, norm_ffn_a[l], bf(ffn_a_in[l]), bf(ffn_a_out[l]), pre=pre)
        if l % 2 == 0:
            r = l // 2
            u_pool, u_x, u_gate = _proj(x, norm_mix[l], bf(rec_w_in[r]), 3)
            weights = _rec_weights(pool_w[r], pool_scale[r], conv_w[r], conv_b[r], gate_r_w[r], gate_r_b[r],
                                   gate_i_w[r], gate_i_b[r], lru_lambda[r])
            mix_p, st_pool, st_conv, st_h = _rec_prompt(u_pool, u_x, u_gate, weights, nb, t)
            mix_s, new_pool, new_conv, new_h = _rec_sample(
                u_pool, u_x, u_gate, state_pool[r].reshape(ns, POOL_HIST * dp),
                state_conv[r].reshape(ns, (CONV_WIDTH - 1) * dl), state_h[r], weights, n_p, past_len)
            pool_p.append(st_pool[:, POOL_CARRY - POOL_HIST:])
            conv_p.append(st_conv[:, CONV_CARRY - (CONV_WIDTH - 1):])
            h_p.append(st_h[:, 0])
            pool_s.append(new_pool.reshape(ns, POOL_HIST, dp))
            conv_s.append(new_conv.reshape(ns, CONV_WIDTH - 1, dl))
            h_s.append(new_h)
            pre = (jnp.concatenate([mix_p, mix_s], axis=0), bf(rec_w_out[r]))
        else:
            a = l // 2
            q, k, v = _proj(x, norm_mix[l], bf(attn_w_qkv[a]), 3)
            o_p = _attn_prompt(q, k, v, attn_logit_bias[a], nb, t)
            o_s = _attn_decode(q, cache_k, cache_v, a, page_table, attn_logit_bias[a], n_p)
            k_p.append(k[:n_p].reshape(nb, t, nh, HEAD_DIM))
            v_p.append(v[:n_p].reshape(nb, t, nh, HEAD_DIM))
            k_s.append(k[n_p:].reshape(ns, 1, nh, HEAD_DIM))
            v_s.append(v[n_p:].reshape(ns, 1, nh, HEAD_DIM))
            pre = (jnp.concatenate([o_p, o_s], axis=0), bf(attn_w_out[a]))
        final = norm_final if l == depth - 1 else None
        x = _ffn(x, norm_ffn_b[l], bf(ffn_b_in[l]), bf(ffn_b_out[l]), pre=pre, final_g=final)
        pre = None
    y_prompt = x[:n_p].reshape(nb, t, d)
    y_sample = x[n_p:].reshape(ns, 1, d)
    return (y_prompt, y_sample, jnp.stack(k_p), jnp.stack(v_p), jnp.stack(k_s), jnp.stack(v_s),
            jnp.stack(pool_p), jnp.stack(conv_p), jnp.stack(h_p), jnp.stack(pool_s), jnp.stack(conv_s),
            jnp.stack(h_s))
```

```python
import functools

import jax
import jax.numpy as jnp
from jax import lax
from jax.experimental import pallas as pl
from jax.experimental.pallas import tpu as pltpu

F32 = jnp.float32
BF16 = jnp.bfloat16

RMS_EPS = 1e-6
FFN_RESIDUAL = 0.5
POOL_WINDOWS = (2, 4, 8, 16)
POOL_HIST = max(POOL_WINDOWS) - 1
CONV_WIDTH = 4
LRU_C = 8.0
HEAD_DIM = 64

V7X_VMEM_BYTES = 64 * 2**20
V7X_MXU_DIM = 256
LANES = 128
SUBLANES = 8
VMEM_LIMIT = V7X_VMEM_BYTES - 8 * 2**20

TOKEN_TILE_CAP = 512
REC_TILE = 256
ATTN_TILE = V7X_MXU_DIM
ATTN_Q_TILE = 2 * ATTN_TILE
FFN_CHUNK = 1024
DECODE_PAGES_PER_STEP = 4
POOL_CARRY = 16
CONV_CARRY = 8


def _pick_tile(n, cap):
    best = SUBLANES
    for t in range(SUBLANES, cap + 1, SUBLANES):
        if n % t == 0:
            best = t
    return best


def _params(*sem):
    return pltpu.CompilerParams(dimension_semantics=sem, vmem_limit_bytes=VMEM_LIMIT)


def _const_spec(shape):
    nd = len(shape)
    return pl.BlockSpec(shape, lambda *_: (0,) * nd, pipeline_mode=pl.Buffered(1))


def _rms(x, g):
    ms = jnp.mean(x * x, axis=-1, keepdims=True)
    return x * lax.rsqrt(ms + RMS_EPS) * g


def _dot(a, b):
    return jnp.dot(a, b, preferred_element_type=F32)


def _softplus(z):
    return jnp.maximum(z, 0.0) + jnp.log(1.0 + jnp.exp(-jnp.abs(z)))


LOG2E = 1.4426950408889634


def _softplus_log2(z2):
    neg_abs = lax.bitcast_convert_type(lax.bitcast_convert_type(z2, jnp.uint32) | jnp.uint32(0x80000000), F32)
    return jnp.maximum(z2, 0.0) + jnp.log(1.0 + jnp.exp2(neg_abs)) * LOG2E


def _blockdiag(w, grp):
    h, c, d = w.shape
    w = w.reshape(h // grp, grp, c, d)
    eye = jnp.eye(grp, dtype=w.dtype)
    return jnp.einsum('kgcd,gh->kgchd', w, eye).reshape(h // grp, grp * c, grp * d)


def _ffn_body(*refs, chunks, has_pre, has_final):
    it = iter(refs)
    x_ref = next(it)
    if has_pre:
        o_ref, wo_ref = next(it), next(it)
    g_ref, win_ref, wout_ref = next(it), next(it), next(it)
    if has_final:
        gf_ref = next(it)
    out_ref = next(it)

    x = x_ref[...]
    if has_pre:
        x = x + _dot(o_ref[...], wo_ref[...])
    h = _rms(x, g_ref[...]).astype(BF16)
    dff = wout_ref.shape[0]
    acc = None
    for s, n in chunks:
        gate = _dot(h, win_ref[:, s:s + n])
        up = _dot(h, win_ref[:, dff + s:dff + s + n])
        act = (gate * jax.nn.sigmoid(gate) * up).astype(BF16)
        part = _dot(act, wout_ref[s:s + n, :])
        acc = part if acc is None else acc + part
    y = x + FFN_RESIDUAL * acc
    if has_final:
        y = _rms(y, gf_ref[...])
    out_ref[...] = y


def _ffn(x, g, w_in, w_out, pre=None, final_g=None):
    n, d = x.shape
    dff = w_out.shape[0]
    tm = _pick_tile(n, TOKEN_TILE_CAP)
    chunks = tuple((s, min(FFN_CHUNK, dff - s)) for s in range(0, dff, FFN_CHUNK))
    row = lambda width: pl.BlockSpec((tm, width), lambda i: (i, 0))
    args, specs = [x], [row(d)]
    if pre is not None:
        o, wo = pre
        args += [o, wo]
        specs += [row(o.shape[1]), _const_spec(wo.shape)]
    args += [g.reshape(1, d), w_in, w_out]
    specs += [_const_spec((1, d)), _const_spec(w_in.shape), _const_spec(w_out.shape)]
    if final_g is not None:
        args.append(final_g.reshape(1, d))
        specs.append(_const_spec((1, d)))
    body = functools.partial(_ffn_body, chunks=chunks, has_pre=pre is not None, has_final=final_g is not None)
    return pl.pallas_call(
        body,
        grid=(n // tm,),
        in_specs=specs,
        out_specs=row(d),
        out_shape=jax.ShapeDtypeStruct((n, d), F32),
        input_output_aliases={0: 0},
        compiler_params=_params("arbitrary"),
        name="ffn",
    )(*args)


def _proj_body(x_ref, g_ref, w_ref, *out_refs):
    h = _rms(x_ref[...], g_ref[...]).astype(BF16)
    wd = w_ref.shape[1] // len(out_refs)
    for k, o in enumerate(out_refs):
        o[...] = _dot(h, w_ref[:, k * wd:(k + 1) * wd])


def _proj(x, g, w, n_out, rows=None):
    d = x.shape[1]
    row0, n = (0, x.shape[0]) if rows is None else rows
    wd = w.shape[1] // n_out
    tm = _pick_tile(n, TOKEN_TILE_CAP)
    assert row0 % tm == 0
    blk0 = row0 // tm
    return pl.pallas_call(
        _proj_body,
        grid=(n // tm,),
        in_specs=[pl.BlockSpec((tm, d), lambda i: (blk0 + i, 0)), _const_spec((1, d)), _const_spec(w.shape)],
        out_specs=[pl.BlockSpec((tm, wd), lambda i: (i, 0))] * n_out,
        out_shape=[jax.ShapeDtypeStruct((n, wd), F32)] * n_out,
        compiler_params=_params("arbitrary"),
        name="proj",
    )(x, g.reshape(1, d), w)


def _grouped_dot(x, w_ref):
    blk = w_ref.shape[1]
    xb = x.astype(BF16)
    return jnp.concatenate([_dot(xb[:, k * blk:(k + 1) * blk], w_ref[k]) for k in range(w_ref.shape[0])], axis=-1)


def _log_sigmoid(x):
    return jnp.minimum(x, 0.0) - jnp.log(1.0 + jnp.exp(-jnp.abs(x)))


def _lru_coeffs(conv, wr_ref, br_ref, wi_ref, bi_ref, lam_ref):
    r = jax.nn.sigmoid(_grouped_dot(conv, wr_ref) + br_ref[...])
    i = jax.nn.sigmoid(_grouped_dot(conv, wi_ref) + bi_ref[...])
    log_a = LRU_C * r * _log_sigmoid(lam_ref[...])
    a = jnp.exp(log_a)
    b = jnp.sqrt(1.0 - jnp.exp(2.0 * log_a)) * (i * conv)
    return a, b


def _rec_prompt_body(up_ref, ux_ref, ug_ref, wp_ref, ps_ref, cw_ref, cb_ref, wr_ref, br_ref, wi_ref, bi_ref,
                     lam_ref, mix_ref, pst_ref, cst_ref, hst_ref, pext, cext, a_scr, b_scr, hs_scr, h_scr, *, tm):
    i = pl.program_id(1)
    dp = up_ref.shape[1]
    pg = dp // len(POOL_WINDOWS)

    @pl.when(i == 0)
    def _():
        pext[0:POOL_CARRY, :] = jnp.zeros((POOL_CARRY, dp), F32)
        cext[0:CONV_CARRY, :] = jnp.zeros((CONV_CARRY, cext.shape[1]), F32)
        h_scr[...] = jnp.zeros(h_scr.shape, F32)

    up = up_ref[...]
    pext[POOL_CARRY:POOL_CARRY + tm, :] = up
    pos = i * tm + lax.broadcasted_iota(jnp.int32, (tm, 1), 0)
    pooled = []
    for g, w in enumerate(POOL_WINDOWS):
        cols = slice(g * pg, (g + 1) * pg)
        s = up[:, cols]
        for j in range(1, w):
            s = s + pext[POOL_CARRY - j:POOL_CARRY - j + tm, cols]
        cnt = jnp.minimum(w, pos + 1).astype(F32)
        pooled.append(s / cnt - up[:, cols])
    pool_out = _grouped_dot(jnp.concatenate(pooled, axis=-1), wp_ref) * ps_ref[...]

    ux = ux_ref[...]
    cext[CONV_CARRY:CONV_CARRY + tm, :] = ux
    conv = cb_ref[...] + ux * cw_ref[CONV_WIDTH - 1:CONV_WIDTH, :]
    for j in range(CONV_WIDTH - 1):
        off = CONV_CARRY - (CONV_WIDTH - 1) + j
        conv = conv + cext[off:off + tm, :] * cw_ref[j:j + 1, :]

    a, b = _lru_coeffs(conv, wr_ref, br_ref, wi_ref, bi_ref, lam_ref)
    a_scr[...] = a
    b_scr[...] = b

    def step(t, h):
        h = a_scr[pl.ds(t, 1), :] * h + b_scr[pl.ds(t, 1), :]
        hs_scr[pl.ds(t, 1), :] = h
        return h

    h_last = lax.fori_loop(0, tm, step, h_scr[0:1, :], unroll=8)
    h_scr[...] = jnp.broadcast_to(h_last, h_scr.shape)

    lru_out = hs_scr[...] * jax.nn.gelu(ug_ref[...])
    mix_ref[:, 0:dp] = pool_out.astype(mix_ref.dtype)
    mix_ref[:, dp:] = lru_out.astype(mix_ref.dtype)

    pext[0:POOL_CARRY, :] = pext[tm:tm + POOL_CARRY, :]
    cext[0:CONV_CARRY, :] = cext[tm:tm + CONV_CARRY, :]
    pst_ref[...] = pext[0:POOL_CARRY, :]
    cst_ref[...] = cext[0:CONV_CARRY, :]
    hst_ref[...] = h_scr[...]


def _rec_weights(pool_w, pool_scale, conv_w, conv_b, w_r, b_r, w_i, b_i, lam):
    dp = pool_scale.shape[0]
    dl = conv_b.shape[0]
    wp = _blockdiag(pool_w, V7X_MXU_DIM // pool_w.shape[1]).astype(BF16)
    wr = _blockdiag(w_r, V7X_MXU_DIM // w_r.shape[1]).astype(BF16)
    wi = _blockdiag(w_i, V7X_MXU_DIM // w_i.shape[1]).astype(BF16)
    return (wp, pool_scale.reshape(1, dp), conv_w, conv_b.reshape(1, dl), wr, b_r.reshape(1, dl), wi,
            b_i.reshape(1, dl), lam.reshape(1, dl))


def _rec_prompt(u_pool, u_x, u_gate, weights, nb, t):
    dp, dl = u_pool.shape[1], u_x.shape[1]
    tm = _pick_tile(t, REC_TILE)
    nt = t // tm
    row = lambda width: pl.BlockSpec((tm, width), lambda b, i: (b * nt + i, 0))
    state = lambda rows, width: pl.BlockSpec((None, rows, width), lambda b, i: (b, 0, 0))
    return pl.pallas_call(
        functools.partial(_rec_prompt_body, tm=tm),
        grid=(nb, nt),
        in_specs=[row(dp), row(dl), row(dl)] + [_const_spec(w.shape) for w in weights],
        out_specs=[row(dp + dl), state(POOL_CARRY, dp), state(CONV_CARRY, dl), state(SUBLANES, dl)],
        out_shape=[jax.ShapeDtypeStruct((nb * t, dp + dl), BF16),
                   jax.ShapeDtypeStruct((nb, POOL_CARRY, dp), F32),
                   jax.ShapeDtypeStruct((nb, CONV_CARRY, dl), F32),
                   jax.ShapeDtypeStruct((nb, SUBLANES, dl), F32)],
        scratch_shapes=[pltpu.VMEM((tm + POOL_CARRY, dp), F32), pltpu.VMEM((tm + CONV_CARRY, dl), F32),
                        pltpu.VMEM((tm, dl), F32), pltpu.VMEM((tm, dl), F32), pltpu.VMEM((tm, dl), F32),
                        pltpu.VMEM((SUBLANES, dl), F32)],
        compiler_params=_params("arbitrary", "arbitrary"),
        name="rec_prompt",
    )(u_pool, u_x, u_gate, *weights)


def _rec_sample_body(up_ref, ux_ref, ug_ref, sp_ref, sc_ref, sh_ref, wp_ref, ps_ref, cw_ref, cb_ref, wr_ref,
                     br_ref, wi_ref, bi_ref, lam_ref, mix_ref, np_ref, nc_ref, nh_ref, *, start_pos):
    dp = up_ref.shape[1]
    dl = ux_ref.shape[1]
    pg = dp // len(POOL_WINDOWS)
    up = up_ref[...]
    pooled = []
    for g, w in enumerate(POOL_WINDOWS):
        s = up[:, g * pg:(g + 1) * pg]
        for j in range(1, w):
            base = (POOL_HIST - j) * dp + g * pg
            s = s + sp_ref[:, base:base + pg]
        pooled.append(s / float(min(w, start_pos + 1)) - up[:, g * pg:(g + 1) * pg])
    pool_out = _grouped_dot(jnp.concatenate(pooled, axis=-1), wp_ref) * ps_ref[...]
    np_ref[:, 0:(POOL_HIST - 1) * dp] = sp_ref[:, dp:]
    np_ref[:, (POOL_HIST - 1) * dp:] = up

    ux = ux_ref[...]
    conv = cb_ref[...] + ux * cw_ref[CONV_WIDTH - 1:CONV_WIDTH, :]
    for j in range(CONV_WIDTH - 1):
        conv = conv + sc_ref[:, j * dl:(j + 1) * dl] * cw_ref[j:j + 1, :]
    nc_ref[:, 0:(CONV_WIDTH - 2) * dl] = sc_ref[:, dl:]
    nc_ref[:, (CONV_WIDTH - 2) * dl:] = ux

    a, b = _lru_coeffs(conv, wr_ref, br_ref, wi_ref, bi_ref, lam_ref)
    h = a * sh_ref[...] + b
    nh_ref[...] = h
    mix_ref[:, 0:dp] = pool_out.astype(mix_ref.dtype)
    mix_ref[:, dp:] = (h * jax.nn.gelu(ug_ref[...])).astype(mix_ref.dtype)


def _rec_sample(u_pool, u_x, u_gate, s_pool, s_conv, s_h, weights, row0, start_pos):
    ns = s_h.shape[0]
    dp, dl = u_pool.shape[1], u_x.shape[1]
    blk = row0 // ns
    rows = lambda width: pl.BlockSpec((ns, width), lambda i: (blk, 0))
    full = lambda a: pl.BlockSpec(a.shape, lambda i: (0,) * a.ndim)
    outs = [jax.ShapeDtypeStruct((ns, dp + dl), BF16), jax.ShapeDtypeStruct(s_pool.shape, F32),
            jax.ShapeDtypeStruct(s_conv.shape, F32), jax.ShapeDtypeStruct(s_h.shape, F32)]
    return pl.pallas_call(
        functools.partial(_rec_sample_body, start_pos=start_pos),
        grid=(1,),
        in_specs=[rows(dp), rows(dl), rows(dl), full(s_pool), full(s_conv), full(s_h)] + [full(w) for w in weights],
        out_specs=[pl.BlockSpec(o.shape, lambda i: (0, 0)) for o in outs],
        out_shape=outs,
        compiler_params=_params("arbitrary"),
        name="rec_sample",
    )(u_pool, u_x, u_gate, s_pool, s_conv, s_h, *weights)


def _qkv_prompt_body(x_ref, g_ref, wq_ref, wkt_ref, wvt_ref, wv_ref, q_ref, ktf_ref, ktb_ref, vtf_ref, vb_ref):
    nt_dims = (((1,), (1,)), ((), ()))
    h = _rms(x_ref[...], g_ref[...]).astype(BF16)
    q_ref[...] = (_dot(h, wq_ref[...]) * (HEAD_DIM ** -0.5 * LOG2E)).astype(q_ref.dtype)
    kt = lax.dot_general(wkt_ref[...], h, nt_dims, preferred_element_type=F32)
    ktf_ref[...] = kt
    tile = ktb_ref.shape[2]
    for s in range(ktb_ref.shape[0]):
        ktb_ref[s] = kt[:, s * tile:(s + 1) * tile].astype(ktb_ref.dtype)
    vtf_ref[...] = lax.dot_general(wvt_ref[...], h, nt_dims, preferred_element_type=F32)
    vb_ref[...] = _dot(h, wv_ref[...]).astype(vb_ref.dtype)


def _qkv_prompt(x, g, w_qkv, nb, t):
    d = x.shape[1]
    tm = _pick_tile(t, TOKEN_TILE_CAP)
    tile = ATTN_TILE
    assert tm % tile == 0
    nt = t // tm
    wq, wk, wv = (w_qkv[:, k * d:(k + 1) * d] for k in range(3))
    row = pl.BlockSpec((tm, d), lambda b, i: (b * nt + i, 0))
    col = pl.BlockSpec((None, d, tm), lambda b, i: (b, 0, i))
    return pl.pallas_call(
        _qkv_prompt_body,
        grid=(nb, nt),
        in_specs=[row, _const_spec((1, d))] + [_const_spec((d, d))] * 4,
        out_specs=[row, col, pl.BlockSpec((None, tm // tile, d, tile), lambda b, i: (b, i, 0, 0)), col, row],
        out_shape=[jax.ShapeDtypeStruct((nb * t, d), BF16), jax.ShapeDtypeStruct((nb, d, t), F32),
                   jax.ShapeDtypeStruct((nb, t // tile, d, tile), BF16), jax.ShapeDtypeStruct((nb, d, t), F32),
                   jax.ShapeDtypeStruct((nb * t, d), BF16)],
        compiler_params=_params("arbitrary", "arbitrary"),
        name="qkv_prompt",
    )(x, g.reshape(1, d), wq, wk.T, wv.T, wv)


MASKED_LOGIT = -1e30


def _attn_prompt_body(bias_ref, q_ref, kt_ref, v_ref, tri_ref, o_ref, z_scr, sp_scr, e_scr, acc_scr, carry_scr,
                      *, tile, q_tile):
    hp = pl.program_id(1)
    qi = pl.program_id(2)
    lane = lax.broadcasted_iota(jnp.int32, (1, 2 * HEAD_DIM), 1)
    q = q_ref[...]
    qh = [jnp.where((lane // HEAD_DIM) == h, q, jnp.zeros_like(q)) for h in range(2)]
    bias = [bias_ref[2 * hp + h] * LOG2E for h in range(2)]

    def stage_a(par, j, masked):
        kblk = kt_ref[j]
        for h in range(2):
            z = _dot(qh[h], kblk) + bias[h]
            if masked:
                q_pos = qi * q_tile + lax.broadcasted_iota(jnp.int32, (q_tile, tile), 0)
                k_pos = j * tile + lax.broadcasted_iota(jnp.int32, (q_tile, tile), 1)
                z = jnp.where(k_pos < q_pos, z, MASKED_LOGIT)
            z_scr[par, h] = z
            sp_scr[par, h] = _softplus_log2(z).astype(sp_scr.dtype)

    def stage_b(par):
        for h in range(2):
            local = _dot(sp_scr[par, h], tri_ref[...])
            carry = carry_scr[h]
            e_scr[par, h] = z_scr[par, h] - (local + jnp.concatenate([carry] * (tile // carry.shape[1]), axis=1))
            carry_scr[h] = carry + local[:, 0:1]

    def stage_c(par, j):
        start = pl.multiple_of(j * tile, tile)
        vblk = v_ref[pl.ds(start, tile), :]
        for h in range(2):
            w = jnp.exp2(e_scr[par, h])
            acc_scr[h] += _dot(w.astype(vblk.dtype), vblk)

    acc_scr[...] = jnp.zeros(acc_scr.shape, F32)
    carry_scr[...] = jnp.zeros(carry_scr.shape, F32)
    n = 2 * (qi + 1)
    top = n - 1
    stage_a(0, top, True)
    stage_b(0)
    stage_a(1, top - 1, True)

    def two_steps(k, _):
        i = 2 * k
        stage_c(0, top - i + 2)
        stage_b(1)
        stage_a(0, top - i, False)
        stage_c(1, top - i + 1)
        stage_b(0)
        stage_a(1, top - i - 1, False)
        return 0

    lax.fori_loop(1, qi + 1, two_steps, 0)
    stage_c(0, 1)
    stage_b(1)
    stage_c(1, 0)
    o_ref[...] = jnp.where(lane < HEAD_DIM, acc_scr[0], acc_scr[1]).astype(o_ref.dtype)


def _attn_prompt(q, kt_blocks, v, bias, nb, t):
    d = q.shape[1]
    tile, q_tile = ATTN_TILE, ATTN_Q_TILE
    assert q_tile == 2 * tile and t % q_tile == 0
    nq = t // q_tile
    width = 2 * HEAD_DIM
    tri = jnp.tril(jnp.ones((tile, tile), BF16))
    return pl.pallas_call(
        functools.partial(_attn_prompt_body, tile=tile, q_tile=q_tile),
        grid=(nb, d // width, nq),
        in_specs=[pl.BlockSpec(memory_space=pltpu.SMEM),
                  pl.BlockSpec((q_tile, width), lambda b, hp, qi: (b * nq + qi, hp)),
                  pl.BlockSpec((None, t // tile, width, tile), lambda b, hp, qi: (b, 0, hp, 0)),
                  pl.BlockSpec((t, width), lambda b, hp, qi: (b, hp)),
                  _const_spec((tile, tile))],
        out_specs=pl.BlockSpec((q_tile, width), lambda b, hp, qi: (b * nq + qi, hp)),
        out_shape=jax.ShapeDtypeStruct((nb * t, d), BF16),
        scratch_shapes=[pltpu.VMEM((2, 2, q_tile, tile), F32), pltpu.VMEM((2, 2, q_tile, tile), BF16),
                        pltpu.VMEM((2, 2, q_tile, tile), F32), pltpu.VMEM((2, q_tile, width), F32),
                        pltpu.VMEM((2, q_tile, width), F32)],
        compiler_params=_params("arbitrary", "arbitrary", "arbitrary"),
        name="attn_prompt",
    )(bias, q, kt_blocks, v, tri)


def _attn_decode_body(pt_ref, q_ref, bias_ref, tri_ref, *refs, group, nh):
    k_refs, v_refs = refs[:group], refs[group:2 * group]
    o_ref, acc_scr, carry_scr = refs[2 * group:]
    n = pl.program_id(0)
    p = pl.program_id(1)
    d, page = acc_scr.shape
    hd = d // nh

    @pl.when((n == 0) & (p == 0))
    def _():
        o_ref[...] = jnp.zeros(o_ref.shape, F32)

    @pl.when(p == 0)
    def _():
        acc_scr[...] = jnp.zeros(acc_scr.shape, F32)
        carry_scr[...] = jnp.zeros(carry_scr.shape, F32)

    q = q_ref[...]
    tri = tri_ref[...]
    for g in range(group):
        z = (k_refs[g][...] * q).reshape(nh, hd, page).sum(axis=1) + bias_ref[...]
        sp = _softplus(z)
        hi = sp.astype(BF16)
        lo = (sp - hi.astype(F32)).astype(BF16)
        sums = _dot(hi, tri) + _dot(lo, tri)
        carry = carry_scr[...]
        w = jnp.exp(z - (sums[:, :page] + carry))
        carry_scr[...] = carry + sums[:, page:]
        v = v_refs[g][...].reshape(nh, hd, page)
        acc_scr[...] += (v * w[:, None, :]).reshape(d, page)

    @pl.when(p == pl.num_programs(1) - 1)
    def _():
        col = jnp.sum(acc_scr[...], axis=1, keepdims=True)
        onehot = lax.broadcasted_iota(jnp.int32, (1, o_ref.shape[1]), 1) == n
        o_ref[...] += jnp.where(onehot, col, 0.0)


def _paged_cache_view(cache):
    nl, nphys, page, nh, hd = cache.shape
    return jnp.transpose(cache, (0, 1, 3, 4, 2)).reshape(nl, nphys, nh * hd, page)


def _attn_decode(q, cache_kt, cache_vt, layer, page_table, bias):
    ns, npages = page_table.shape
    d, page = cache_kt.shape[2:]
    nh = bias.shape[0]
    group = max(g for g in range(1, DECODE_PAGES_PER_STEP + 1) if npages % g == 0)
    q_rep = jnp.broadcast_to((q * (HEAD_DIM ** -0.5))[:, :, None], (ns, d, page))
    bias_rep = jnp.broadcast_to(bias[:, None], (nh, page)).astype(F32)
    tri = jnp.concatenate([jnp.tril(jnp.ones((page, page), BF16)), jnp.ones((page, page), BF16)], axis=1)

    def page_spec(g):
        return pl.BlockSpec((None, None, d, page),
                            lambda n, p, pt: (layer, pt[n * npages + npages - 1 - (p * group + g)], 0, 0))

    const = lambda shape: pl.BlockSpec(shape, lambda n, p, pt: (0,) * len(shape))
    pages = [page_spec(g) for g in range(group)]
    grid_spec = pltpu.PrefetchScalarGridSpec(
        num_scalar_prefetch=1,
        grid=(ns, npages // group),
        in_specs=[pl.BlockSpec((None, d, page), lambda n, p, pt: (n, 0, 0)), const((nh, page)),
                  const((page, 2 * page))] + pages + pages,
        out_specs=pl.BlockSpec((d, ns), lambda n, p, pt: (0, 0)),
        scratch_shapes=[pltpu.VMEM((d, page), F32), pltpu.VMEM((nh, page), F32)],
    )
    out_t = pl.pallas_call(
        functools.partial(_attn_decode_body, group=group, nh=nh),
        grid_spec=grid_spec,
        out_shape=jax.ShapeDtypeStruct((d, ns), F32),
        compiler_params=_params("arbitrary", "arbitrary"),
        name="attn_decode",
    )(page_table.reshape(-1), q_rep, bias_rep, tri, *([cache_kt] * group), *([cache_vt] * group))
    return out_t.T


def kernel(x_prompt, x_sample, cache_k, cache_v, page_table, state_pool, state_conv, state_h, norm_ffn_a, ffn_a_in, ffn_a_out, norm_mix, norm_ffn_b, ffn_b_in, ffn_b_out, rec_w_in, pool_w, pool_scale, conv_w, conv_b, gate_r_w, gate_r_b, gate_i_w, gate_i_b, lru_lambda, rec_w_out, attn_w_qkv, attn_logit_bias, attn_w_out, norm_final):
    nb, t, d = x_prompt.shape
    ns, t_dec, _ = x_sample.shape
    assert t_dec == 1 and (nb * t) % ns == 0 and t % ATTN_TILE == 0
    depth = norm_mix.shape[0]
    nh = d // HEAD_DIM
    past_len = page_table.shape[1] * cache_k.shape[2]
    n_p = nb * t
    dp, dl = pool_scale.shape[1], conv_b.shape[1]

    x = jnp.concatenate([x_prompt.reshape(n_p, d), x_sample.reshape(ns, d)], axis=0)
    bf = lambda w: w.astype(BF16)

    k_p, v_p, k_s, v_s = [], [], [], []
    pool_p, conv_p, h_p, pool_s, conv_s, h_s = [], [], [], [], [], []
    cache_kt, cache_vt = _paged_cache_view(cache_k), _paged_cache_view(cache_v)
    to_heads = lambda xt: jnp.transpose(xt.reshape(nb, nh, HEAD_DIM, t), (0, 3, 1, 2))
    for l in range(depth):
        x = _ffn(x, norm_ffn_a[l], bf(ffn_a_in[l]), bf(ffn_a_out[l]))
        if l % 2 == 0:
            r = l // 2
            u_pool, u_x, u_gate = _proj(x, norm_mix[l], bf(rec_w_in[r]), 3)
            weights = _rec_weights(pool_w[r], pool_scale[r], conv_w[r], conv_b[r], gate_r_w[r], gate_r_b[r],
                                   gate_i_w[r], gate_i_b[r], lru_lambda[r])
            mix_p, st_pool, st_conv, st_h = _rec_prompt(u_pool, u_x, u_gate, weights, nb, t)
            mix_s, new_pool, new_conv, new_h = _rec_sample(
                u_pool, u_x, u_gate, state_pool[r].reshape(ns, POOL_HIST * dp),
                state_conv[r].reshape(ns, (CONV_WIDTH - 1) * dl), state_h[r], weights, n_p, past_len)
            pool_p.append(st_pool[:, POOL_CARRY - POOL_HIST:])
            conv_p.append(st_conv[:, CONV_CARRY - (CONV_WIDTH - 1):])
            h_p.append(st_h[:, 0])
            pool_s.append(new_pool.reshape(ns, POOL_HIST, dp))
            conv_s.append(new_conv.reshape(ns, CONV_WIDTH - 1, dl))
            h_s.append(new_h)
            pre = (jnp.concatenate([mix_p, mix_s], axis=0), bf(rec_w_out[r]))
        else:
            a = l // 2
            w_qkv = bf(attn_w_qkv[a])
            q_bf, kt, kt_blocks, vt, v_bf = _qkv_prompt(x, norm_mix[l], w_qkv, nb, t)
            q_s, k_new, v_new = _proj(x, norm_mix[l], w_qkv, 3, rows=(n_p, ns))
            o_p = _attn_prompt(q_bf, kt_blocks, v_bf, attn_logit_bias[a], nb, t)
            o_s = _attn_decode(q_s, cache_kt, cache_vt, a, page_table, attn_logit_bias[a])
            k_p.append(to_heads(kt))
            v_p.append(to_heads(vt))
            k_s.append(k_new.reshape(ns, 1, nh, HEAD_DIM))
            v_s.append(v_new.reshape(ns, 1, nh, HEAD_DIM))
            pre = (jnp.concatenate([o_p, o_s.astype(BF16)], axis=0), bf(attn_w_out[a]))
        final = norm_final if l == depth - 1 else None
        x = _ffn(x, norm_ffn_b[l], bf(ffn_b_in[l]), bf(ffn_b_out[l]), pre=pre, final_g=final)
    y_prompt = x[:n_p].reshape(nb, t, d)
    y_sample = x[n_p:].reshape(ns, 1, d)
    return (y_prompt, y_sample, jnp.stack(k_p), jnp.stack(v_p), jnp.stack(k_s), jnp.stack(v_s),
            jnp.stack(pool_p), jnp.stack(conv_p), jnp.stack(h_p), jnp.stack(pool_s), jnp.stack(conv_s),
            jnp.stack(h_s))
```

```python
import functools

import jax
import jax.numpy as jnp
from jax import lax
from jax.experimental import pallas as pl
from jax.experimental.pallas import tpu as pltpu

F32 = jnp.float32
BF16 = jnp.bfloat16

RMS_EPS = 1e-6
FFN_RESIDUAL = 0.5
POOL_WINDOWS = (2, 4, 8, 16)
POOL_HIST = max(POOL_WINDOWS) - 1
CONV_WIDTH = 4
LRU_C = 8.0
HEAD_DIM = 64

V7X_VMEM_BYTES = 64 * 2**20
V7X_MXU_DIM = 256
LANES = 128
SUBLANES = 8
VMEM_LIMIT = V7X_VMEM_BYTES - 8 * 2**20

TOKEN_TILE_CAP = 512
REC_TILE = 256
ATTN_TILE = V7X_MXU_DIM
ATTN_Q_TILE = 2 * ATTN_TILE
ATTN_HEADS = 8
FFN_CHUNK = 1024
DECODE_PAGES_PER_STEP = 8
POOL_CARRY = 16
CONV_CARRY = 8


def _pick_tile(n, cap):
    best = SUBLANES
    for t in range(SUBLANES, cap + 1, SUBLANES):
        if n % t == 0:
            best = t
    return best


def _params(*sem):
    return pltpu.CompilerParams(dimension_semantics=sem, vmem_limit_bytes=VMEM_LIMIT)


def _const_spec(shape):
    nd = len(shape)
    return pl.BlockSpec(shape, lambda *_: (0,) * nd, pipeline_mode=pl.Buffered(1))


def _rms(x, g):
    ms = jnp.mean(x * x, axis=-1, keepdims=True)
    return x * lax.rsqrt(ms + RMS_EPS) * g


def _dot(a, b):
    return jnp.dot(a, b, preferred_element_type=F32)


def _softplus(z):
    return jnp.maximum(z, 0.0) + jnp.log(1.0 + jnp.exp(-jnp.abs(z)))


LOG2E = 1.4426950408889634


def _softplus_log2(z2):
    neg_abs = lax.bitcast_convert_type(lax.bitcast_convert_type(z2, jnp.uint32) | jnp.uint32(0x80000000), F32)
    return jnp.maximum(z2, 0.0) + jnp.log(1.0 + jnp.exp2(neg_abs)) * LOG2E


def _blockdiag(w, grp):
    h, c, d = w.shape
    w = w.reshape(h // grp, grp, c, d)
    eye = jnp.eye(grp, dtype=w.dtype)
    return jnp.einsum('kgcd,gh->kgchd', w, eye).reshape(h // grp, grp * c, grp * d)


def _ffn_body(*refs, chunks, has_pre, has_final):
    it = iter(refs)
    x_ref = next(it)
    if has_pre:
        o_ref, wo_ref = next(it), next(it)
    g_ref, win_ref, wout_ref = next(it), next(it), next(it)
    if has_final:
        gf_ref = next(it)
    out_ref = next(it)

    x = x_ref[...]
    if has_pre:
        x = x + _dot(o_ref[...], wo_ref[...])
    h = _rms(x, g_ref[...]).astype(BF16)
    dff = wout_ref.shape[0]
    acc = None
    for s, n in chunks:
        gate = _dot(h, win_ref[:, s:s + n])
        up = _dot(h, win_ref[:, dff + s:dff + s + n])
        act = (gate * jax.nn.sigmoid(gate) * up).astype(BF16)
        part = _dot(act, wout_ref[s:s + n, :])
        acc = part if acc is None else acc + part
    y = x + FFN_RESIDUAL * acc
    if has_final:
        y = _rms(y, gf_ref[...])
    out_ref[...] = y


def _ffn(x, g, w_in, w_out, pre=None, final_g=None):
    n, d = x.shape
    dff = w_out.shape[0]
    tm = _pick_tile(n, TOKEN_TILE_CAP)
    chunks = tuple((s, min(FFN_CHUNK, dff - s)) for s in range(0, dff, FFN_CHUNK))
    row = lambda width: pl.BlockSpec((tm, width), lambda i: (i, 0))
    args, specs = [x], [row(d)]
    if pre is not None:
        o, wo = pre
        args += [o, wo]
        specs += [row(o.shape[1]), _const_spec(wo.shape)]
    args += [g.reshape(1, d), w_in, w_out]
    specs += [_const_spec((1, d)), _const_spec(w_in.shape), _const_spec(w_out.shape)]
    if final_g is not None:
        args.append(final_g.reshape(1, d))
        specs.append(_const_spec((1, d)))
    body = functools.partial(_ffn_body, chunks=chunks, has_pre=pre is not None, has_final=final_g is not None)
    return pl.pallas_call(
        body,
        grid=(n // tm,),
        in_specs=specs,
        out_specs=row(d),
        out_shape=jax.ShapeDtypeStruct((n, d), F32),
        input_output_aliases={0: 0},
        compiler_params=_params("arbitrary"),
        name="ffn",
    )(*args)


def _proj_body(x_ref, g_ref, w_ref, *out_refs):
    h = _rms(x_ref[...], g_ref[...]).astype(BF16)
    wd = w_ref.shape[1] // len(out_refs)
    for k, o in enumerate(out_refs):
        o[...] = _dot(h, w_ref[:, k * wd:(k + 1) * wd])


def _proj(x, g, w, n_out, rows=None):
    d = x.shape[1]
    row0, n = (0, x.shape[0]) if rows is None else rows
    wd = w.shape[1] // n_out
    tm = _pick_tile(n, TOKEN_TILE_CAP)
    assert row0 % tm == 0
    blk0 = row0 // tm
    return pl.pallas_call(
        _proj_body,
        grid=(n // tm,),
        in_specs=[pl.BlockSpec((tm, d), lambda i: (blk0 + i, 0)), _const_spec((1, d)), _const_spec(w.shape)],
        out_specs=[pl.BlockSpec((tm, wd), lambda i: (i, 0))] * n_out,
        out_shape=[jax.ShapeDtypeStruct((n, wd), F32)] * n_out,
        compiler_params=_params("arbitrary"),
        name="proj",
    )(x, g.reshape(1, d), w)


def _grouped_dot(x, w_ref):
    blk = w_ref.shape[1]
    xb = x.astype(BF16)
    return jnp.concatenate([_dot(xb[:, k * blk:(k + 1) * blk], w_ref[k]) for k in range(w_ref.shape[0])], axis=-1)


def _log_sigmoid(x):
    return jnp.minimum(x, 0.0) - jnp.log(1.0 + jnp.exp(-jnp.abs(x)))


def _lru_coeffs(conv, wr_ref, br_ref, wi_ref, bi_ref, lam_ref):
    r = jax.nn.sigmoid(_grouped_dot(conv, wr_ref) + br_ref[...])
    i = jax.nn.sigmoid(_grouped_dot(conv, wi_ref) + bi_ref[...])
    log_a = LRU_C * r * _log_sigmoid(lam_ref[...])
    a = jnp.exp(log_a)
    b = jnp.sqrt(1.0 - jnp.exp(2.0 * log_a)) * (i * conv)
    return a, b


def _rec_prompt_body(up_ref, ux_ref, ug_ref, wp_ref, ps_ref, cw_ref, cb_ref, wr_ref, br_ref, wi_ref, bi_ref,
                     lam_ref, mix_ref, pst_ref, cst_ref, hst_ref, pext, cext, a_scr, b_scr, hs_scr, h_scr, *, tm):
    i = pl.program_id(1)
    dp = up_ref.shape[1]
    pg = dp // len(POOL_WINDOWS)

    @pl.when(i == 0)
    def _():
        pext[0:POOL_CARRY, :] = jnp.zeros((POOL_CARRY, dp), F32)
        cext[0:CONV_CARRY, :] = jnp.zeros((CONV_CARRY, cext.shape[1]), F32)
        h_scr[...] = jnp.zeros(h_scr.shape, F32)

    up = up_ref[...]
    pext[POOL_CARRY:POOL_CARRY + tm, :] = up
    pos = i * tm + lax.broadcasted_iota(jnp.int32, (tm, 1), 0)
    pooled = []
    for g, w in enumerate(POOL_WINDOWS):
        cols = slice(g * pg, (g + 1) * pg)
        s = up[:, cols]
        for j in range(1, w):
            s = s + pext[POOL_CARRY - j:POOL_CARRY - j + tm, cols]
        cnt = jnp.minimum(w, pos + 1).astype(F32)
        pooled.append(s / cnt - up[:, cols])
    pool_out = _grouped_dot(jnp.concatenate(pooled, axis=-1), wp_ref) * ps_ref[...]

    ux = ux_ref[...]
    cext[CONV_CARRY:CONV_CARRY + tm, :] = ux
    conv = cb_ref[...] + ux * cw_ref[CONV_WIDTH - 1:CONV_WIDTH, :]
    for j in range(CONV_WIDTH - 1):
        off = CONV_CARRY - (CONV_WIDTH - 1) + j
        conv = conv + cext[off:off + tm, :] * cw_ref[j:j + 1, :]

    a, b = _lru_coeffs(conv, wr_ref, br_ref, wi_ref, bi_ref, lam_ref)
    a_scr[...] = a
    b_scr[...] = b

    def step(t, h):
        h = a_scr[pl.ds(t, 1), :] * h + b_scr[pl.ds(t, 1), :]
        hs_scr[pl.ds(t, 1), :] = h
        return h

    h_last = lax.fori_loop(0, tm, step, h_scr[0:1, :], unroll=8)
    h_scr[...] = jnp.broadcast_to(h_last, h_scr.shape)

    lru_out = hs_scr[...] * jax.nn.gelu(ug_ref[...])
    mix_ref[:, 0:dp] = pool_out.astype(mix_ref.dtype)
    mix_ref[:, dp:] = lru_out.astype(mix_ref.dtype)

    pext[0:POOL_CARRY, :] = pext[tm:tm + POOL_CARRY, :]
    cext[0:CONV_CARRY, :] = cext[tm:tm + CONV_CARRY, :]
    pst_ref[...] = pext[0:POOL_CARRY, :]
    cst_ref[...] = cext[0:CONV_CARRY, :]
    hst_ref[...] = h_scr[...]


def _rec_weights(pool_w, pool_scale, conv_w, conv_b, w_r, b_r, w_i, b_i, lam):
    dp = pool_scale.shape[0]
    dl = conv_b.shape[0]
    wp = _blockdiag(pool_w, V7X_MXU_DIM // pool_w.shape[1]).astype(BF16)
    wr = _blockdiag(w_r, V7X_MXU_DIM // w_r.shape[1]).astype(BF16)
    wi = _blockdiag(w_i, V7X_MXU_DIM // w_i.shape[1]).astype(BF16)
    return (wp, pool_scale.reshape(1, dp), conv_w, conv_b.reshape(1, dl), wr, b_r.reshape(1, dl), wi,
            b_i.reshape(1, dl), lam.reshape(1, dl))


def _rec_prompt(u_pool, u_x, u_gate, weights, nb, t):
    dp, dl = u_pool.shape[1], u_x.shape[1]
    tm = _pick_tile(t, REC_TILE)
    nt = t // tm
    row = lambda width: pl.BlockSpec((tm, width), lambda b, i: (b * nt + i, 0))
    state = lambda rows, width: pl.BlockSpec((None, rows, width), lambda b, i: (b, 0, 0))
    return pl.pallas_call(
        functools.partial(_rec_prompt_body, tm=tm),
        grid=(nb, nt),
        in_specs=[row(dp), row(dl), row(dl)] + [_const_spec(w.shape) for w in weights],
        out_specs=[row(dp + dl), state(POOL_CARRY, dp), state(CONV_CARRY, dl), state(SUBLANES, dl)],
        out_shape=[jax.ShapeDtypeStruct((nb * t, dp + dl), BF16),
                   jax.ShapeDtypeStruct((nb, POOL_CARRY, dp), F32),
                   jax.ShapeDtypeStruct((nb, CONV_CARRY, dl), F32),
                   jax.ShapeDtypeStruct((nb, SUBLANES, dl), F32)],
        scratch_shapes=[pltpu.VMEM((tm + POOL_CARRY, dp), F32), pltpu.VMEM((tm + CONV_CARRY, dl), F32),
                        pltpu.VMEM((tm, dl), F32), pltpu.VMEM((tm, dl), F32), pltpu.VMEM((tm, dl), F32),
                        pltpu.VMEM((SUBLANES, dl), F32)],
        compiler_params=_params("arbitrary", "arbitrary"),
        name="rec_prompt",
    )(u_pool, u_x, u_gate, *weights)


def _rec_sample_body(up_ref, ux_ref, ug_ref, sp_ref, sc_ref, sh_ref, wp_ref, ps_ref, cw_ref, cb_ref, wr_ref,
                     br_ref, wi_ref, bi_ref, lam_ref, mix_ref, np_ref, nc_ref, nh_ref, *, start_pos):
    dp = up_ref.shape[1]
    dl = ux_ref.shape[1]
    pg = dp // len(POOL_WINDOWS)
    up = up_ref[...]
    pooled = []
    for g, w in enumerate(POOL_WINDOWS):
        s = up[:, g * pg:(g + 1) * pg]
        for j in range(1, w):
            base = (POOL_HIST - j) * dp + g * pg
            s = s + sp_ref[:, base:base + pg]
        pooled.append(s / float(min(w, start_pos + 1)) - up[:, g * pg:(g + 1) * pg])
    pool_out = _grouped_dot(jnp.concatenate(pooled, axis=-1), wp_ref) * ps_ref[...]
    np_ref[:, 0:(POOL_HIST - 1) * dp] = sp_ref[:, dp:]
    np_ref[:, (POOL_HIST - 1) * dp:] = up

    ux = ux_ref[...]
    conv = cb_ref[...] + ux * cw_ref[CONV_WIDTH - 1:CONV_WIDTH, :]
    for j in range(CONV_WIDTH - 1):
        conv = conv + sc_ref[:, j * dl:(j + 1) * dl] * cw_ref[j:j + 1, :]
    nc_ref[:, 0:(CONV_WIDTH - 2) * dl] = sc_ref[:, dl:]
    nc_ref[:, (CONV_WIDTH - 2) * dl:] = ux

    a, b = _lru_coeffs(conv, wr_ref, br_ref, wi_ref, bi_ref, lam_ref)
    h = a * sh_ref[...] + b
    nh_ref[...] = h
    mix_ref[:, 0:dp] = pool_out.astype(mix_ref.dtype)
    mix_ref[:, dp:] = (h * jax.nn.gelu(ug_ref[...])).astype(mix_ref.dtype)


def _rec_sample(u_pool, u_x, u_gate, s_pool, s_conv, s_h, weights, row0, start_pos):
    ns = s_h.shape[0]
    dp, dl = u_pool.shape[1], u_x.shape[1]
    blk = row0 // ns
    rows = lambda width: pl.BlockSpec((ns, width), lambda i: (blk, 0))
    full = lambda a: pl.BlockSpec(a.shape, lambda i: (0,) * a.ndim)
    outs = [jax.ShapeDtypeStruct((ns, dp + dl), BF16), jax.ShapeDtypeStruct(s_pool.shape, F32),
            jax.ShapeDtypeStruct(s_conv.shape, F32), jax.ShapeDtypeStruct(s_h.shape, F32)]
    return pl.pallas_call(
        functools.partial(_rec_sample_body, start_pos=start_pos),
        grid=(1,),
        in_specs=[rows(dp), rows(dl), rows(dl), full(s_pool), full(s_conv), full(s_h)] + [full(w) for w in weights],
        out_specs=[pl.BlockSpec(o.shape, lambda i: (0, 0)) for o in outs],
        out_shape=outs,
        compiler_params=_params("arbitrary"),
        name="rec_sample",
    )(u_pool, u_x, u_gate, s_pool, s_conv, s_h, *weights)


def _qkv_prompt_body(x_ref, g_ref, wq_ref, wkt_ref, wvt_ref, wv_ref, q_ref, ktf_ref, ktb_ref, vtf_ref, vb_ref):
    nt_dims = (((1,), (1,)), ((), ()))
    h = _rms(x_ref[...], g_ref[...]).astype(BF16)
    q_ref[...] = (_dot(h, wq_ref[...]) * (HEAD_DIM ** -0.5 * LOG2E)).astype(q_ref.dtype)
    kt = lax.dot_general(wkt_ref[...], h, nt_dims, preferred_element_type=F32)
    ktf_ref[...] = kt
    tile = ktb_ref.shape[2]
    for s in range(ktb_ref.shape[0]):
        ktb_ref[s] = kt[:, s * tile:(s + 1) * tile].astype(ktb_ref.dtype)
    vtf_ref[...] = lax.dot_general(wvt_ref[...], h, nt_dims, preferred_element_type=F32)
    vb_ref[...] = _dot(h, wv_ref[...]).astype(vb_ref.dtype)


def _qkv_prompt(x, g, w_qkv, nb, t):
    d = x.shape[1]
    tm = _pick_tile(t, TOKEN_TILE_CAP)
    tile = ATTN_TILE
    assert tm % tile == 0
    nt = t // tm
    wq, wk, wv = (w_qkv[:, k * d:(k + 1) * d] for k in range(3))
    row = pl.BlockSpec((tm, d), lambda b, i: (b * nt + i, 0))
    col = pl.BlockSpec((None, d, tm), lambda b, i: (b, 0, i))
    return pl.pallas_call(
        _qkv_prompt_body,
        grid=(nb, nt),
        in_specs=[row, _const_spec((1, d))] + [_const_spec((d, d))] * 4,
        out_specs=[row, col, pl.BlockSpec((None, tm // tile, d, tile), lambda b, i: (b, i, 0, 0)), col, row],
        out_shape=[jax.ShapeDtypeStruct((nb * t, d), BF16), jax.ShapeDtypeStruct((nb, d, t), F32),
                   jax.ShapeDtypeStruct((nb, t // tile, d, tile), BF16), jax.ShapeDtypeStruct((nb, d, t), F32),
                   jax.ShapeDtypeStruct((nb * t, d), BF16)],
        compiler_params=_params("arbitrary", "arbitrary"),
        name="qkv_prompt",
    )(x, g.reshape(1, d), wq, wk.T, wv.T, wv)


MASKED_LOGIT = -1e30


def _attn_prompt_body(bias_ref, q_ref, kt_ref, v_ref, tri_ref, o_ref, z_scr, sp_scr, e_scr, acc_scr, carry_scr,
                      *, tile, q_tile, heads):
    hg = pl.program_id(1)
    qi = pl.program_id(2)
    pair = 2 * HEAD_DIM
    lane = lax.broadcasted_iota(jnp.int32, (1, pair), 1)
    groups = [slice((h // 2) * pair, (h // 2 + 1) * pair) for h in range(heads)]
    qh = []
    for h in range(heads):
        qg = q_ref[:, groups[h]]
        qh.append(jnp.where((lane // HEAD_DIM) == h % 2, qg, jnp.zeros_like(qg)))
    bias = [bias_ref[heads * hg + h] * LOG2E for h in range(heads)]

    def stage_a(par, j, masked):
        for h in range(heads):
            z = _dot(qh[h], kt_ref[j, groups[h], :]) + bias[h]
            if masked:
                q_pos = qi * q_tile + lax.broadcasted_iota(jnp.int32, (q_tile, tile), 0)
                k_pos = j * tile + lax.broadcasted_iota(jnp.int32, (q_tile, tile), 1)
                z = jnp.where(k_pos < q_pos, z, MASKED_LOGIT)
            z_scr[par, h] = z
            sp_scr[par, h] = _softplus_log2(z).astype(sp_scr.dtype)

    def stage_b(par):
        for h in range(heads):
            local = _dot(sp_scr[par, h], tri_ref[...])
            carry = carry_scr[h]
            e_scr[par, h] = z_scr[par, h] - (local + jnp.concatenate([carry] * (tile // carry.shape[1]), axis=1))
            carry_scr[h] = carry + local[:, 0:1]

    def stage_c(par, j):
        start = pl.multiple_of(j * tile, tile)
        for h in range(heads):
            vblk = v_ref[pl.ds(start, tile), groups[h]]
            w = jnp.exp2(e_scr[par, h])
            acc_scr[h] += _dot(w.astype(vblk.dtype), vblk)

    acc_scr[...] = jnp.zeros(acc_scr.shape, F32)
    carry_scr[...] = jnp.zeros(carry_scr.shape, F32)
    n = 2 * (qi + 1)
    top = n - 1
    stage_a(0, top, True)
    stage_b(0)
    stage_a(1, top - 1, True)

    def two_steps(k, _):
        i = 2 * k
        stage_c(0, top - i + 2)
        stage_b(1)
        stage_a(0, top - i, False)
        stage_c(1, top - i + 1)
        stage_b(0)
        stage_a(1, top - i - 1, False)
        return 0

    lax.fori_loop(1, qi + 1, two_steps, 0)
    stage_c(0, 1)
    stage_b(1)
    stage_c(1, 0)
    for g in range(heads // 2):
        o_ref[:, g * pair:(g + 1) * pair] = jnp.where(lane < HEAD_DIM, acc_scr[2 * g],
                                                      acc_scr[2 * g + 1]).astype(o_ref.dtype)


def _attn_prompt(q, kt_blocks, v, bias, nb, t):
    d = q.shape[1]
    tile, q_tile, heads = ATTN_TILE, ATTN_Q_TILE, ATTN_HEADS
    assert q_tile == 2 * tile and t % q_tile == 0 and heads % 2 == 0
    nq = t // q_tile
    width = heads * HEAD_DIM
    tri = jnp.tril(jnp.ones((tile, tile), BF16))
    stage_buf = lambda dtype: pltpu.VMEM((2, heads, q_tile, tile), dtype)
    row_buf = pltpu.VMEM((heads, q_tile, 2 * HEAD_DIM), F32)
    return pl.pallas_call(
        functools.partial(_attn_prompt_body, tile=tile, q_tile=q_tile, heads=heads),
        grid=(nb, d // width, nq),
        in_specs=[pl.BlockSpec(memory_space=pltpu.SMEM),
                  pl.BlockSpec((q_tile, width), lambda b, hg, qi: (b * nq + qi, hg)),
                  pl.BlockSpec((None, t // tile, width, tile), lambda b, hg, qi: (b, 0, hg, 0),
                               pipeline_mode=pl.Buffered(1)),
                  pl.BlockSpec((t, width), lambda b, hg, qi: (b, hg), pipeline_mode=pl.Buffered(1)),
                  _const_spec((tile, tile))],
        out_specs=pl.BlockSpec((q_tile, width), lambda b, hg, qi: (b * nq + qi, hg)),
        out_shape=jax.ShapeDtypeStruct((nb * t, d), BF16),
        scratch_shapes=[stage_buf(F32), stage_buf(BF16), stage_buf(F32), row_buf, row_buf],
        compiler_params=_params("arbitrary", "arbitrary", "arbitrary"),
        name="attn_prompt",
    )(bias, q, kt_blocks, v, tri)


def _attn_decode_body(pt_ref, q_ref, bias_ref, tri_ref, *refs, group, nh):
    k_refs, v_refs = refs[:group], refs[group:2 * group]
    o_ref, acc_scr, carry_scr = refs[2 * group:]
    n = pl.program_id(0)
    p = pl.program_id(1)
    d, page = acc_scr.shape
    hd = d // nh

    @pl.when((n == 0) & (p == 0))
    def _():
        o_ref[...] = jnp.zeros(o_ref.shape, F32)

    @pl.when(p == 0)
    def _():
        acc_scr[...] = jnp.zeros(acc_scr.shape, F32)
        carry_scr[...] = jnp.zeros(carry_scr.shape, F32)

    q = q_ref[...]
    tri = tri_ref[...]
    for g in range(group):
        z = (k_refs[g][...] * q).reshape(nh, hd, page).sum(axis=1) + bias_ref[...]
        sp = _softplus(z)
        hi = sp.astype(BF16)
        lo = (sp - hi.astype(F32)).astype(BF16)
        sums = _dot(hi, tri) + _dot(lo, tri)
        carry = carry_scr[...]
        w = jnp.exp(z - (sums[:, :page] + carry))
        carry_scr[...] = carry + sums[:, page:]
        v = v_refs[g][...].reshape(nh, hd, page)
        acc_scr[...] += (v * w[:, None, :]).reshape(d, page)

    @pl.when(p == pl.num_programs(1) - 1)
    def _():
        col = jnp.sum(acc_scr[...], axis=1, keepdims=True)
        onehot = lax.broadcasted_iota(jnp.int32, (1, o_ref.shape[1]), 1) == n
        o_ref[...] += jnp.where(onehot, col, 0.0)


def _paged_cache_view(cache):
    nl, nphys, page, nh, hd = cache.shape
    return jnp.transpose(cache, (0, 1, 3, 4, 2)).reshape(nl, nphys, nh * hd, page)


def _attn_decode(q, cache_kt, cache_vt, layer, page_table, bias):
    ns, npages = page_table.shape
    d, page = cache_kt.shape[2:]
    nh = bias.shape[0]
    group = max(g for g in range(1, DECODE_PAGES_PER_STEP + 1) if npages % g == 0)
    q_rep = jnp.broadcast_to((q * (HEAD_DIM ** -0.5))[:, :, None], (ns, d, page))
    bias_rep = jnp.broadcast_to(bias[:, None], (nh, page)).astype(F32)
    tri = jnp.concatenate([jnp.tril(jnp.ones((page, page), BF16)), jnp.ones((page, page), BF16)], axis=1)

    def page_spec(g):
        return pl.BlockSpec((None, None, d, page),
                            lambda n, p, pt: (layer, pt[n * npages + npages - 1 - (p * group + g)], 0, 0))

    const = lambda shape: pl.BlockSpec(shape, lambda n, p, pt: (0,) * len(shape))
    pages = [page_spec(g) for g in range(group)]
    grid_spec = pltpu.PrefetchScalarGridSpec(
        num_scalar_prefetch=1,
        grid=(ns, npages // group),
        in_specs=[pl.BlockSpec((None, d, page), lambda n, p, pt: (n, 0, 0)), const((nh, page)),
                  const((page, 2 * page))] + pages + pages,
        out_specs=pl.BlockSpec((d, ns), lambda n, p, pt: (0, 0)),
        scratch_shapes=[pltpu.VMEM((d, page), F32), pltpu.VMEM((nh, page), F32)],
    )
    out_t = pl.pallas_call(
        functools.partial(_attn_decode_body, group=group, nh=nh),
        grid_spec=grid_spec,
        out_shape=jax.ShapeDtypeStruct((d, ns), F32),
        compiler_params=_params("arbitrary", "arbitrary"),
        name="attn_decode",
    )(page_table.reshape(-1), q_rep, bias_rep, tri, *([cache_kt] * group), *([cache_vt] * group))
    return out_t.T


def kernel(x_prompt, x_sample, cache_k, cache_v, page_table, state_pool, state_conv, state_h, norm_ffn_a, ffn_a_in, ffn_a_out, norm_mix, norm_ffn_b, ffn_b_in, ffn_b_out, rec_w_in, pool_w, pool_scale, conv_w, conv_b, gate_r_w, gate_r_b, gate_i_w, gate_i_b, lru_lambda, rec_w_out, attn_w_qkv, attn_logit_bias, attn_w_out, norm_final):
    nb, t, d = x_prompt.shape
    ns, t_dec, _ = x_sample.shape
    assert t_dec == 1 and (nb * t) % ns == 0 and t % ATTN_TILE == 0
    depth = norm_mix.shape[0]
    nh = d // HEAD_DIM
    past_len = page_table.shape[1] * cache_k.shape[2]
    n_p = nb * t
    dp, dl = pool_scale.shape[1], conv_b.shape[1]

    x = jnp.concatenate([x_prompt.reshape(n_p, d), x_sample.reshape(ns, d)], axis=0)
    bf = lambda w: w.astype(BF16)

    k_p, v_p, k_s, v_s = [], [], [], []
    pool_p, conv_p, h_p, pool_s, conv_s, h_s = [], [], [], [], [], []
    cache_kt, cache_vt = _paged_cache_view(cache_k), _paged_cache_view(cache_v)
    to_heads = lambda xt: jnp.transpose(xt.reshape(nb, nh, HEAD_DIM, t), (0, 3, 1, 2))
    for l in range(depth):
        x = _ffn(x, norm_ffn_a[l], bf(ffn_a_in[l]), bf(ffn_a_out[l]))
        if l % 2 == 0:
            r = l // 2
            u_pool, u_x, u_gate = _proj(x, norm_mix[l], bf(rec_w_in[r]), 3)
            weights = _rec_weights(pool_w[r], pool_scale[r], conv_w[r], conv_b[r], gate_r_w[r], gate_r_b[r],
                                   gate_i_w[r], gate_i_b[r], lru_lambda[r])
            mix_p, st_pool, st_conv, st_h = _rec_prompt(u_pool, u_x, u_gate, weights, nb, t)
            mix_s, new_pool, new_conv, new_h = _rec_sample(
                u_pool, u_x, u_gate, state_pool[r].reshape(ns, POOL_HIST * dp),
                state_conv[r].reshape(ns, (CONV_WIDTH - 1) * dl), state_h[r], weights, n_p, past_len)
            pool_p.append(st_pool[:, POOL_CARRY - POOL_HIST:])
            conv_p.append(st_conv[:, CONV_CARRY - (CONV_WIDTH - 1):])
            h_p.append(st_h[:, 0])
            pool_s.append(new_pool.reshape(ns, POOL_HIST, dp))
            conv_s.append(new_conv.reshape(ns, CONV_WIDTH - 1, dl))
            h_s.append(new_h)
            pre = (jnp.concatenate([mix_p, mix_s], axis=0), bf(rec_w_out[r]))
        else:
            a = l // 2
            w_qkv = bf(attn_w_qkv[a])
            q_bf, kt, kt_blocks, vt, v_bf = _qkv_prompt(x, norm_mix[l], w_qkv, nb, t)
            q_s, k_new, v_new = _proj(x, norm_mix[l], w_qkv, 3, rows=(n_p, ns))
            o_p = _attn_prompt(q_bf, kt_blocks, v_bf, attn_logit_bias[a], nb, t)
            o_s = _attn_decode(q_s, cache_kt, cache_vt, a, page_table, attn_logit_bias[a])
            k_p.append(to_heads(kt))
            v_p.append(to_heads(vt))
            k_s.append(k_new.reshape(ns, 1, nh, HEAD_DIM))
            v_s.append(v_new.reshape(ns, 1, nh, HEAD_DIM))
            pre = (jnp.concatenate([o_p, o_s.astype(BF16)], axis=0), bf(attn_w_out[a]))
        final = norm_final if l == depth - 1 else None
        x = _ffn(x, norm_ffn_b[l], bf(ffn_b_in[l]), bf(ffn_b_out[l]), pre=pre, final_g=final)
    y_prompt = x[:n_p].reshape(nb, t, d)
    y_sample = x[n_p:].reshape(ns, 1, d)
    return (y_prompt, y_sample, jnp.stack(k_p), jnp.stack(v_p), jnp.stack(k_s), jnp.stack(v_s),
            jnp.stack(pool_p), jnp.stack(conv_p), jnp.stack(h_p), jnp.stack(pool_s), jnp.stack(conv_s),
            jnp.stack(h_s))
```

```python
import functools

import jax
import jax.numpy as jnp
from jax import lax
from jax.experimental import pallas as pl
from jax.experimental.pallas import tpu as pltpu

F32 = jnp.float32
BF16 = jnp.bfloat16

RMS_EPS = 1e-6
FFN_RESIDUAL = 0.5
POOL_WINDOWS = (2, 4, 8, 16)
POOL_HIST = max(POOL_WINDOWS) - 1
CONV_WIDTH = 4
LRU_C = 8.0
HEAD_DIM = 64

V7X_VMEM_BYTES = 64 * 2**20
V7X_MXU_DIM = 256
LANES = 128
SUBLANES = 8
VMEM_LIMIT = V7X_VMEM_BYTES - 8 * 2**20

TOKEN_TILE_CAP = 512
REC_TILE = 256
ATTN_TILE = V7X_MXU_DIM
ATTN_Q_TILE = 2 * ATTN_TILE
ATTN_HEADS = 8
FFN_CHUNK = 1024
DECODE_PAGES_PER_STEP = 8
POOL_CARRY = 16
CONV_CARRY = 8


def _pick_tile(n, cap):
    best = SUBLANES
    for t in range(SUBLANES, cap + 1, SUBLANES):
        if n % t == 0:
            best = t
    return best


def _params(*sem):
    return pltpu.CompilerParams(dimension_semantics=sem, vmem_limit_bytes=VMEM_LIMIT)


def _const_spec(shape):
    nd = len(shape)
    return pl.BlockSpec(shape, lambda *_: (0,) * nd, pipeline_mode=pl.Buffered(1))


def _rms(x, g):
    ms = jnp.mean(x * x, axis=-1, keepdims=True)
    return x * lax.rsqrt(ms + RMS_EPS) * g


def _dot(a, b):
    return jnp.dot(a, b, preferred_element_type=F32)


def _softplus(z):
    return jnp.maximum(z, 0.0) + jnp.log(1.0 + jnp.exp(-jnp.abs(z)))


LOG2E = 1.4426950408889634


def _softplus_log2(z2):
    neg_abs = lax.bitcast_convert_type(lax.bitcast_convert_type(z2, jnp.uint32) | jnp.uint32(0x80000000), F32)
    return jnp.maximum(z2, 0.0) + jnp.log(1.0 + jnp.exp2(neg_abs)) * LOG2E


def _blockdiag(w, grp):
    h, c, d = w.shape
    w = w.reshape(h // grp, grp, c, d)
    eye = jnp.eye(grp, dtype=w.dtype)
    return jnp.einsum('kgcd,gh->kgchd', w, eye).reshape(h // grp, grp * c, grp * d)


def _joined_rows(main_ref, tail_ref, tail_rows):
    a = main_ref[...]
    keep = a.shape[0] - tail_rows
    mixed = jnp.concatenate([a[:keep], tail_ref[...]], axis=0)
    return jnp.where(pl.program_id(0) == pl.num_programs(0) - 1, mixed, a)


def _ffn_body(*refs, chunks, x_tail, pre_tail, has_pre, has_final):
    it = iter(refs)
    x_ref = next(it)
    xs_ref = next(it) if x_tail else None
    if has_pre:
        o_ref = next(it)
        os_ref = next(it) if pre_tail else None
        wo_ref = next(it)
    g_ref, win_ref, wout_ref = next(it), next(it), next(it)
    if has_final:
        gf_ref = next(it)
    out_ref = next(it)

    x = _joined_rows(x_ref, xs_ref, x_tail) if x_tail else x_ref[...]
    if has_pre:
        o = _joined_rows(o_ref, os_ref, pre_tail) if pre_tail else o_ref[...]
        x = x + _dot(o, wo_ref[...])
    h = _rms(x, g_ref[...]).astype(BF16)
    dff = wout_ref.shape[0]
    acc = None
    for s, n in chunks:
        gate = _dot(h, win_ref[:, s:s + n])
        up = _dot(h, win_ref[:, dff + s:dff + s + n])
        act = (gate * jax.nn.sigmoid(gate) * up).astype(BF16)
        part = _dot(act, wout_ref[s:s + n, :])
        acc = part if acc is None else acc + part
    y = x + FFN_RESIDUAL * acc
    if has_final:
        y = _rms(y, gf_ref[...])
    out_ref[...] = y


def _ffn(x, g, w_in, w_out, pre=None, final_g=None):
    def split(a):
        return (a[0], a[1], a[1].shape[0]) if isinstance(a, tuple) else (a, None, 0)

    x_main, x_tail, x_tail_rows = split(x)
    n, d = x_main.shape[0] + x_tail_rows, x_main.shape[1]
    dff = w_out.shape[0]
    tm = _pick_tile(n, TOKEN_TILE_CAP)
    chunks = tuple((s, min(FFN_CHUNK, dff - s)) for s in range(0, dff, FFN_CHUNK))
    row = lambda width: pl.BlockSpec((tm, width), lambda i: (i, 0))
    whole = lambda a: pl.BlockSpec(a.shape, lambda i: (0, 0))

    def rows_args(main, tail, tail_rows):
        if tail is None:
            return [main], [row(main.shape[1])]
        assert tail_rows < tm and (main.shape[0] + tail_rows) % tm == 0
        return [main, tail], [row(main.shape[1]), whole(tail)]

    args, specs = rows_args(x_main, x_tail, x_tail_rows)
    pre_tail_rows = 0
    if pre is not None:
        o, wo = pre
        o_main, o_tail, pre_tail_rows = split(o)
        a, s = rows_args(o_main, o_tail, pre_tail_rows)
        args += a + [wo]
        specs += s + [_const_spec(wo.shape)]
    args += [g.reshape(1, d), w_in, w_out]
    specs += [_const_spec((1, d)), _const_spec(w_in.shape), _const_spec(w_out.shape)]
    if final_g is not None:
        args.append(final_g.reshape(1, d))
        specs.append(_const_spec((1, d)))
    body = functools.partial(_ffn_body, chunks=chunks, x_tail=x_tail_rows, pre_tail=pre_tail_rows,
                             has_pre=pre is not None, has_final=final_g is not None)
    return pl.pallas_call(
        body,
        grid=(n // tm,),
        in_specs=specs,
        out_specs=row(d),
        out_shape=jax.ShapeDtypeStruct((n, d), F32),
        input_output_aliases={} if x_tail is not None else {0: 0},
        compiler_params=_params("arbitrary"),
        name="ffn",
    )(*args)


def _proj_body(x_ref, g_ref, w_ref, *out_refs):
    h = _rms(x_ref[...], g_ref[...]).astype(BF16)
    wd = w_ref.shape[1] // len(out_refs)
    for k, o in enumerate(out_refs):
        o[...] = _dot(h, w_ref[:, k * wd:(k + 1) * wd])


def _proj(x, g, w, n_out, rows=None):
    d = x.shape[1]
    row0, n = (0, x.shape[0]) if rows is None else rows
    wd = w.shape[1] // n_out
    tm = _pick_tile(n, TOKEN_TILE_CAP)
    assert row0 % tm == 0
    blk0 = row0 // tm
    return pl.pallas_call(
        _proj_body,
        grid=(n // tm,),
        in_specs=[pl.BlockSpec((tm, d), lambda i: (blk0 + i, 0)), _const_spec((1, d)), _const_spec(w.shape)],
        out_specs=[pl.BlockSpec((tm, wd), lambda i: (i, 0))] * n_out,
        out_shape=[jax.ShapeDtypeStruct((n, wd), F32)] * n_out,
        compiler_params=_params("arbitrary"),
        name="proj",
    )(x, g.reshape(1, d), w)


def _grouped_dot(x, w_ref):
    blk = w_ref.shape[1]
    xb = x.astype(BF16)
    return jnp.concatenate([_dot(xb[:, k * blk:(k + 1) * blk], w_ref[k]) for k in range(w_ref.shape[0])], axis=-1)


def _log_sigmoid(x):
    return jnp.minimum(x, 0.0) - jnp.log(1.0 + jnp.exp(-jnp.abs(x)))


def _lru_coeffs(conv, wr_ref, br_ref, wi_ref, bi_ref, lam_ref):
    r = jax.nn.sigmoid(_grouped_dot(conv, wr_ref) + br_ref[...])
    i = jax.nn.sigmoid(_grouped_dot(conv, wi_ref) + bi_ref[...])
    log_a = LRU_C * r * _log_sigmoid(lam_ref[...])
    a = jnp.exp(log_a)
    b = jnp.sqrt(1.0 - jnp.exp(2.0 * log_a)) * (i * conv)
    return a, b


def _rec_prompt_body(up_ref, ux_ref, ug_ref, wp_ref, ps_ref, cw_ref, cb_ref, wr_ref, br_ref, wi_ref, bi_ref,
                     lam_ref, mix_ref, pst_ref, cst_ref, hst_ref, pext, cext, a_scr, b_scr, hs_scr, h_scr, *, tm):
    i = pl.program_id(1)
    dp = up_ref.shape[1]
    pg = dp // len(POOL_WINDOWS)

    @pl.when(i == 0)
    def _():
        pext[0:POOL_CARRY, :] = jnp.zeros((POOL_CARRY, dp), F32)
        cext[0:CONV_CARRY, :] = jnp.zeros((CONV_CARRY, cext.shape[1]), F32)
        h_scr[...] = jnp.zeros(h_scr.shape, F32)

    up = up_ref[...]
    pext[POOL_CARRY:POOL_CARRY + tm, :] = up
    pos = i * tm + lax.broadcasted_iota(jnp.int32, (tm, 1), 0)
    pooled = []
    for g, w in enumerate(POOL_WINDOWS):
        cols = slice(g * pg, (g + 1) * pg)
        s = up[:, cols]
        for j in range(1, w):
            s = s + pext[POOL_CARRY - j:POOL_CARRY - j + tm, cols]
        cnt = jnp.minimum(w, pos + 1).astype(F32)
        pooled.append(s / cnt - up[:, cols])
    pool_out = _grouped_dot(jnp.concatenate(pooled, axis=-1), wp_ref) * ps_ref[...]

    ux = ux_ref[...]
    cext[CONV_CARRY:CONV_CARRY + tm, :] = ux
    conv = cb_ref[...] + ux * cw_ref[CONV_WIDTH - 1:CONV_WIDTH, :]
    for j in range(CONV_WIDTH - 1):
        off = CONV_CARRY - (CONV_WIDTH - 1) + j
        conv = conv + cext[off:off + tm, :] * cw_ref[j:j + 1, :]

    a, b = _lru_coeffs(conv, wr_ref, br_ref, wi_ref, bi_ref, lam_ref)
    a_scr[...] = a
    b_scr[...] = b

    def step(t, h):
        h = a_scr[pl.ds(t, 1), :] * h + b_scr[pl.ds(t, 1), :]
        hs_scr[pl.ds(t, 1), :] = h
        return h

    h_last = lax.fori_loop(0, tm, step, h_scr[0:1, :], unroll=8)
    h_scr[...] = jnp.broadcast_to(h_last, h_scr.shape)

    lru_out = hs_scr[...] * jax.nn.gelu(ug_ref[...])
    mix_ref[:, 0:dp] = pool_out.astype(mix_ref.dtype)
    mix_ref[:, dp:] = lru_out.astype(mix_ref.dtype)

    pext[0:POOL_CARRY, :] = pext[tm:tm + POOL_CARRY, :]
    cext[0:CONV_CARRY, :] = cext[tm:tm + CONV_CARRY, :]
    pst_ref[...] = pext[0:POOL_CARRY, :]
    cst_ref[...] = cext[0:CONV_CARRY, :]
    hst_ref[...] = h_scr[...]


def _rec_weights(pool_w, pool_scale, conv_w, conv_b, w_r, b_r, w_i, b_i, lam):
    dp = pool_scale.shape[0]
    dl = conv_b.shape[0]
    wp = _blockdiag(pool_w, V7X_MXU_DIM // pool_w.shape[1]).astype(BF16)
    wr = _blockdiag(w_r, V7X_MXU_DIM // w_r.shape[1]).astype(BF16)
    wi = _blockdiag(w_i, V7X_MXU_DIM // w_i.shape[1]).astype(BF16)
    return (wp, pool_scale.reshape(1, dp), conv_w, conv_b.reshape(1, dl), wr, b_r.reshape(1, dl), wi,
            b_i.reshape(1, dl), lam.reshape(1, dl))


def _rec_prompt(u_pool, u_x, u_gate, weights, nb, t):
    dp, dl = u_pool.shape[1], u_x.shape[1]
    tm = _pick_tile(t, REC_TILE)
    nt = t // tm
    row = lambda width: pl.BlockSpec((tm, width), lambda b, i: (b * nt + i, 0))
    state = lambda rows, width: pl.BlockSpec((None, rows, width), lambda b, i: (b, 0, 0))
    return pl.pallas_call(
        functools.partial(_rec_prompt_body, tm=tm),
        grid=(nb, nt),
        in_specs=[row(dp), row(dl), row(dl)] + [_const_spec(w.shape) for w in weights],
        out_specs=[row(dp + dl), state(POOL_CARRY, dp), state(CONV_CARRY, dl), state(SUBLANES, dl)],
        out_shape=[jax.ShapeDtypeStruct((nb * t, dp + dl), BF16),
                   jax.ShapeDtypeStruct((nb, POOL_CARRY, dp), F32),
                   jax.ShapeDtypeStruct((nb, CONV_CARRY, dl), F32),
                   jax.ShapeDtypeStruct((nb, SUBLANES, dl), F32)],
        scratch_shapes=[pltpu.VMEM((tm + POOL_CARRY, dp), F32), pltpu.VMEM((tm + CONV_CARRY, dl), F32),
                        pltpu.VMEM((tm, dl), F32), pltpu.VMEM((tm, dl), F32), pltpu.VMEM((tm, dl), F32),
                        pltpu.VMEM((SUBLANES, dl), F32)],
        compiler_params=_params("arbitrary", "arbitrary"),
        name="rec_prompt",
    )(u_pool, u_x, u_gate, *weights)


def _rec_sample_body(up_ref, ux_ref, ug_ref, sp_ref, sc_ref, sh_ref, wp_ref, ps_ref, cw_ref, cb_ref, wr_ref,
                     br_ref, wi_ref, bi_ref, lam_ref, mix_ref, np_ref, nc_ref, nh_ref, *, start_pos):
    dp = up_ref.shape[1]
    dl = ux_ref.shape[1]
    pg = dp // len(POOL_WINDOWS)
    up = up_ref[...]
    pooled = []
    for g, w in enumerate(POOL_WINDOWS):
        s = up[:, g * pg:(g + 1) * pg]
        for j in range(1, w):
            base = (POOL_HIST - j) * dp + g * pg
            s = s + sp_ref[:, base:base + pg]
        pooled.append(s / float(min(w, start_pos + 1)) - up[:, g * pg:(g + 1) * pg])
    pool_out = _grouped_dot(jnp.concatenate(pooled, axis=-1), wp_ref) * ps_ref[...]
    np_ref[:, 0:(POOL_HIST - 1) * dp] = sp_ref[:, dp:]
    np_ref[:, (POOL_HIST - 1) * dp:] = up

    ux = ux_ref[...]
    conv = cb_ref[...] + ux * cw_ref[CONV_WIDTH - 1:CONV_WIDTH, :]
    for j in range(CONV_WIDTH - 1):
        conv = conv + sc_ref[:, j * dl:(j + 1) * dl] * cw_ref[j:j + 1, :]
    nc_ref[:, 0:(CONV_WIDTH - 2) * dl] = sc_ref[:, dl:]
    nc_ref[:, (CONV_WIDTH - 2) * dl:] = ux

    a, b = _lru_coeffs(conv, wr_ref, br_ref, wi_ref, bi_ref, lam_ref)
    h = a * sh_ref[...] + b
    nh_ref[...] = h
    mix_ref[:, 0:dp] = pool_out.astype(mix_ref.dtype)
    mix_ref[:, dp:] = (h * jax.nn.gelu(ug_ref[...])).astype(mix_ref.dtype)


def _rec_sample(u_pool, u_x, u_gate, s_pool, s_conv, s_h, weights, row0, start_pos):
    ns = s_h.shape[0]
    dp, dl = u_pool.shape[1], u_x.shape[1]
    blk = row0 // ns
    rows = lambda width: pl.BlockSpec((ns, width), lambda i: (blk, 0))
    full = lambda a: pl.BlockSpec(a.shape, lambda i: (0,) * a.ndim)
    outs = [jax.ShapeDtypeStruct((ns, dp + dl), BF16), jax.ShapeDtypeStruct(s_pool.shape, F32),
            jax.ShapeDtypeStruct(s_conv.shape, F32), jax.ShapeDtypeStruct(s_h.shape, F32)]
    return pl.pallas_call(
        functools.partial(_rec_sample_body, start_pos=start_pos),
        grid=(1,),
        in_specs=[rows(dp), rows(dl), rows(dl), full(s_pool), full(s_conv), full(s_h)] + [full(w) for w in weights],
        out_specs=[pl.BlockSpec(o.shape, lambda i: (0, 0)) for o in outs],
        out_shape=outs,
        compiler_params=_params("arbitrary"),
        name="rec_sample",
    )(u_pool, u_x, u_gate, s_pool, s_conv, s_h, *weights)


def _qkv_prompt_body(*refs, n_prev):
    x_ref, g_ref, wq_ref, wkt_ref, wvt_ref, wv_ref = refs[:6]
    prev = refs[6:6 + 2 * min(n_prev, 1)]
    q_ref, ktf_ref, ktb_ref, vtf_ref, vb_ref = refs[6 + len(prev):]
    nt_dims = (((1,), (1,)), ((), ()))
    h = _rms(x_ref[...], g_ref[...]).astype(BF16)
    q_ref[...] = (_dot(h, wq_ref[...]) * (HEAD_DIM ** -0.5 * LOG2E)).astype(q_ref.dtype)
    kt = lax.dot_general(wkt_ref[...], h, nt_dims, preferred_element_type=F32)
    tile = ktb_ref.shape[2]
    for s in range(ktb_ref.shape[0]):
        ktb_ref[s] = kt[:, s * tile:(s + 1) * tile].astype(ktb_ref.dtype)
    vb_ref[...] = _dot(h, wv_ref[...]).astype(vb_ref.dtype)
    for l in range(n_prev):
        ktf_ref[l] = prev[0][l]
        vtf_ref[l] = prev[1][l]
    ktf_ref[n_prev] = kt
    vtf_ref[n_prev] = lax.dot_general(wvt_ref[...], h, nt_dims, preferred_element_type=F32)


def _qkv_prompt(x, g, w_qkv, nb, t, prev=None):
    d = x.shape[1]
    tm = _pick_tile(t, TOKEN_TILE_CAP)
    tile = ATTN_TILE
    assert tm % tile == 0
    nt = t // tm
    n_prev = 0 if prev is None else prev[0].shape[0]
    wq, wk, wv = (w_qkv[:, k * d:(k + 1) * d] for k in range(3))
    row = pl.BlockSpec((tm, d), lambda b, i: (b * nt + i, 0))
    stack = lambda layers: pl.BlockSpec((layers, None, d, tm), lambda b, i: (0, b, 0, i))
    stack_shape = jax.ShapeDtypeStruct((n_prev + 1, nb, d, t), F32)
    return pl.pallas_call(
        functools.partial(_qkv_prompt_body, n_prev=n_prev),
        grid=(nb, nt),
        in_specs=[row, _const_spec((1, d))] + [_const_spec((d, d))] * 4 + [stack(n_prev)] * (2 if n_prev else 0),
        out_specs=[row, stack(n_prev + 1), pl.BlockSpec((None, tm // tile, d, tile), lambda b, i: (b, i, 0, 0)),
                   stack(n_prev + 1), row],
        out_shape=[jax.ShapeDtypeStruct((nb * t, d), BF16), stack_shape,
                   jax.ShapeDtypeStruct((nb, t // tile, d, tile), BF16), stack_shape,
                   jax.ShapeDtypeStruct((nb * t, d), BF16)],
        compiler_params=_params("arbitrary", "arbitrary"),
        name="qkv_prompt",
    )(x, g.reshape(1, d), wq, wk.T, wv.T, wv, *(prev or ()))


MASKED_LOGIT = -1e30


def _attn_prompt_body(bias_ref, q_ref, kt_ref, v_ref, tri_ref, o_ref, z_scr, sp_scr, e_scr, acc_scr, carry_scr,
                      *, tile, q_tile, heads):
    hg = pl.program_id(1)
    qi = pl.program_id(2)
    pair = 2 * HEAD_DIM
    lane = lax.broadcasted_iota(jnp.int32, (1, pair), 1)
    groups = [slice((h // 2) * pair, (h // 2 + 1) * pair) for h in range(heads)]
    qh = []
    for h in range(heads):
        qg = q_ref[:, groups[h]]
        qh.append(jnp.where((lane // HEAD_DIM) == h % 2, qg, jnp.zeros_like(qg)))
    bias = [bias_ref[heads * hg + h] * LOG2E for h in range(heads)]

    def stage_a(par, j, masked):
        for h in range(heads):
            z = _dot(qh[h], kt_ref[j, groups[h], :]) + bias[h]
            if masked:
                q_pos = qi * q_tile + lax.broadcasted_iota(jnp.int32, (q_tile, tile), 0)
                k_pos = j * tile + lax.broadcasted_iota(jnp.int32, (q_tile, tile), 1)
                z = jnp.where(k_pos < q_pos, z, MASKED_LOGIT)
            z_scr[par, h] = z
            sp_scr[par, h] = _softplus_log2(z).astype(sp_scr.dtype)

    def stage_b(par):
        for h in range(heads):
            local = _dot(sp_scr[par, h], tri_ref[...])
            carry = carry_scr[h]
            e_scr[par, h] = z_scr[par, h] - (local + jnp.concatenate([carry] * (tile // carry.shape[1]), axis=1))
            carry_scr[h] = carry + local[:, 0:1]

    def stage_c(par, j):
        start = pl.multiple_of(j * tile, tile)
        for h in range(heads):
            vblk = v_ref[pl.ds(start, tile), groups[h]]
            w = jnp.exp2(e_scr[par, h])
            acc_scr[h] += _dot(w.astype(vblk.dtype), vblk)

    acc_scr[...] = jnp.zeros(acc_scr.shape, F32)
    carry_scr[...] = jnp.zeros(carry_scr.shape, F32)
    n = 2 * (qi + 1)
    top = n - 1
    stage_a(0, top, True)
    stage_b(0)
    stage_a(1, top - 1, True)

    def two_steps(k, _):
        i = 2 * k
        stage_c(0, top - i + 2)
        stage_b(1)
        stage_a(0, top - i, False)
        stage_c(1, top - i + 1)
        stage_b(0)
        stage_a(1, top - i - 1, False)
        return 0

    lax.fori_loop(1, qi + 1, two_steps, 0)
    stage_c(0, 1)
    stage_b(1)
    stage_c(1, 0)
    for g in range(heads // 2):
        o_ref[:, g * pair:(g + 1) * pair] = jnp.where(lane < HEAD_DIM, acc_scr[2 * g],
                                                      acc_scr[2 * g + 1]).astype(o_ref.dtype)


def _attn_prompt(q, kt_blocks, v, bias, nb, t):
    d = q.shape[1]
    tile, q_tile, heads = ATTN_TILE, ATTN_Q_TILE, ATTN_HEADS
    assert q_tile == 2 * tile and t % q_tile == 0 and heads % 2 == 0
    nq = t // q_tile
    width = heads * HEAD_DIM
    tri = jnp.tril(jnp.ones((tile, tile), BF16))
    stage_buf = lambda dtype: pltpu.VMEM((2, heads, q_tile, tile), dtype)
    row_buf = pltpu.VMEM((heads, q_tile, 2 * HEAD_DIM), F32)
    return pl.pallas_call(
        functools.partial(_attn_prompt_body, tile=tile, q_tile=q_tile, heads=heads),
        grid=(nb, d // width, nq),
        in_specs=[pl.BlockSpec(memory_space=pltpu.SMEM),
                  pl.BlockSpec((q_tile, width), lambda b, hg, qi: (b * nq + qi, hg)),
                  pl.BlockSpec((None, t // tile, width, tile), lambda b, hg, qi: (b, 0, hg, 0),
                               pipeline_mode=pl.Buffered(1)),
                  pl.BlockSpec((t, width), lambda b, hg, qi: (b, hg), pipeline_mode=pl.Buffered(1)),
                  _const_spec((tile, tile))],
        out_specs=pl.BlockSpec((q_tile, width), lambda b, hg, qi: (b * nq + qi, hg)),
        out_shape=jax.ShapeDtypeStruct((nb * t, d), BF16),
        scratch_shapes=[stage_buf(F32), stage_buf(BF16), stage_buf(F32), row_buf, row_buf],
        compiler_params=_params("arbitrary", "arbitrary", "arbitrary"),
        name="attn_prompt",
    )(bias, q, kt_blocks, v, tri)


def _attn_decode_body(pt_ref, q_ref, bias_ref, tri_ref, *refs, group, nh):
    k_refs, v_refs = refs[:group], refs[group:2 * group]
    o_ref, acc_scr, carry_scr = refs[2 * group:]
    n = pl.program_id(0)
    p = pl.program_id(1)
    d, page = acc_scr.shape
    hd = d // nh

    @pl.when((n == 0) & (p == 0))
    def _():
        o_ref[...] = jnp.zeros(o_ref.shape, F32)

    @pl.when(p == 0)
    def _():
        acc_scr[...] = jnp.zeros(acc_scr.shape, F32)
        carry_scr[...] = jnp.zeros(carry_scr.shape, F32)

    q = q_ref[...]
    tri = tri_ref[...]
    for g in range(group):
        z = (k_refs[g][...] * q).reshape(nh, hd, page).sum(axis=1) + bias_ref[...]
        sp = _softplus(z)
        hi = sp.astype(BF16)
        lo = (sp - hi.astype(F32)).astype(BF16)
        sums = _dot(hi, tri) + _dot(lo, tri)
        carry = carry_scr[...]
        w = jnp.exp(z - (sums[:, :page] + carry))
        carry_scr[...] = carry + sums[:, page:]
        v = v_refs[g][...].reshape(nh, hd, page)
        acc_scr[...] += (v * w[:, None, :]).reshape(d, page)

    @pl.when(p == pl.num_programs(1) - 1)
    def _():
        col = jnp.sum(acc_scr[...], axis=1, keepdims=True)
        onehot = lax.broadcasted_iota(jnp.int32, (1, o_ref.shape[1]), 1) == n
        o_ref[...] += jnp.where(onehot, col, 0.0)


def _paged_cache_view(cache):
    nl, nphys, page, nh, hd = cache.shape
    return jnp.transpose(cache, (0, 1, 3, 4, 2)).reshape(nl, nphys, nh * hd, page)


def _attn_decode(q, cache_kt, cache_vt, layer, page_table, bias):
    ns, npages = page_table.shape
    d, page = cache_kt.shape[2:]
    nh = bias.shape[0]
    group = max(g for g in range(1, DECODE_PAGES_PER_STEP + 1) if npages % g == 0)
    q_rep = jnp.broadcast_to((q * (HEAD_DIM ** -0.5))[:, :, None], (ns, d, page))
    bias_rep = jnp.broadcast_to(bias[:, None], (nh, page)).astype(F32)
    tri = jnp.concatenate([jnp.tril(jnp.ones((page, page), BF16)), jnp.ones((page, page), BF16)], axis=1)

    def page_spec(g):
        return pl.BlockSpec((None, None, d, page),
                            lambda n, p, pt: (layer, pt[n * npages + npages - 1 - (p * group + g)], 0, 0))

    const = lambda shape: pl.BlockSpec(shape, lambda n, p, pt: (0,) * len(shape))
    pages = [page_spec(g) for g in range(group)]
    grid_spec = pltpu.PrefetchScalarGridSpec(
        num_scalar_prefetch=1,
        grid=(ns, npages // group),
        in_specs=[pl.BlockSpec((None, d, page), lambda n, p, pt: (n, 0, 0)), const((nh, page)),
                  const((page, 2 * page))] + pages + pages,
        out_specs=pl.BlockSpec((d, ns), lambda n, p, pt: (0, 0)),
        scratch_shapes=[pltpu.VMEM((d, page), F32), pltpu.VMEM((nh, page), F32)],
    )
    out_t = pl.pallas_call(
        functools.partial(_attn_decode_body, group=group, nh=nh),
        grid_spec=grid_spec,
        out_shape=jax.ShapeDtypeStruct((d, ns), F32),
        compiler_params=_params("arbitrary", "arbitrary"),
        name="attn_decode",
    )(page_table.reshape(-1), q_rep, bias_rep, tri, *([cache_kt] * group), *([cache_vt] * group))
    return out_t.T


def kernel(x_prompt, x_sample, cache_k, cache_v, page_table, state_pool, state_conv, state_h, norm_ffn_a, ffn_a_in, ffn_a_out, norm_mix, norm_ffn_b, ffn_b_in, ffn_b_out, rec_w_in, pool_w, pool_scale, conv_w, conv_b, gate_r_w, gate_r_b, gate_i_w, gate_i_b, lru_lambda, rec_w_out, attn_w_qkv, attn_logit_bias, attn_w_out, norm_final):
    nb, t, d = x_prompt.shape
    ns, t_dec, _ = x_sample.shape
    assert t_dec == 1 and (nb * t) % ns == 0 and t % ATTN_TILE == 0
    depth = norm_mix.shape[0]
    nh = d // HEAD_DIM
    past_len = page_table.shape[1] * cache_k.shape[2]
    n_p = nb * t
    dp, dl = pool_scale.shape[1], conv_b.shape[1]

    x = (x_prompt.reshape(n_p, d), x_sample.reshape(ns, d))
    bf = lambda w: w.astype(BF16)

    k_s, v_s = [], []
    pool_p, conv_p, h_p, pool_s, conv_s, h_s = [], [], [], [], [], []
    cache_kt, cache_vt = _paged_cache_view(cache_k), _paged_cache_view(cache_v)
    kv_t = None
    for l in range(depth):
        x = _ffn(x, norm_ffn_a[l], bf(ffn_a_in[l]), bf(ffn_a_out[l]))
        if l % 2 == 0:
            r = l // 2
            u_pool, u_x, u_gate = _proj(x, norm_mix[l], bf(rec_w_in[r]), 3)
            weights = _rec_weights(pool_w[r], pool_scale[r], conv_w[r], conv_b[r], gate_r_w[r], gate_r_b[r],
                                   gate_i_w[r], gate_i_b[r], lru_lambda[r])
            mix_p, st_pool, st_conv, st_h = _rec_prompt(u_pool, u_x, u_gate, weights, nb, t)
            mix_s, new_pool, new_conv, new_h = _rec_sample(
                u_pool, u_x, u_gate, state_pool[r].reshape(ns, POOL_HIST * dp),
                state_conv[r].reshape(ns, (CONV_WIDTH - 1) * dl), state_h[r], weights, n_p, past_len)
            pool_p.append(st_pool[:, POOL_CARRY - POOL_HIST:])
            conv_p.append(st_conv[:, CONV_CARRY - (CONV_WIDTH - 1):])
            h_p.append(st_h[:, 0])
            pool_s.append(new_pool.reshape(ns, POOL_HIST, dp))
            conv_s.append(new_conv.reshape(ns, CONV_WIDTH - 1, dl))
            h_s.append(new_h)
            pre = ((mix_p, mix_s), bf(rec_w_out[r]))
        else:
            a = l // 2
            w_qkv = bf(attn_w_qkv[a])
            q_bf, kt_all, kt_blocks, vt_all, v_bf = _qkv_prompt(x, norm_mix[l], w_qkv, nb, t, prev=kv_t)
            kv_t = (kt_all, vt_all)
            q_s, k_new, v_new = _proj(x, norm_mix[l], w_qkv, 3, rows=(n_p, ns))
            o_p = _attn_prompt(q_bf, kt_blocks, v_bf, attn_logit_bias[a], nb, t)
            o_s = _attn_decode(q_s, cache_kt, cache_vt, a, page_table, attn_logit_bias[a])
            k_s.append(k_new.reshape(ns, 1, nh, HEAD_DIM))
            v_s.append(v_new.reshape(ns, 1, nh, HEAD_DIM))
            pre = ((o_p, o_s.astype(BF16)), bf(attn_w_out[a]))
        final = norm_final if l == depth - 1 else None
        x = _ffn(x, norm_ffn_b[l], bf(ffn_b_in[l]), bf(ffn_b_out[l]), pre=pre, final_g=final)
    y_prompt = x[:n_p].reshape(nb, t, d)
    y_sample = x[n_p:].reshape(ns, 1, d)
    k_p, v_p = (jnp.transpose(a.reshape(a.shape[0], nb, nh, HEAD_DIM, t), (0, 1, 4, 2, 3)) for a in kv_t)
    return (y_prompt, y_sample, k_p, v_p, jnp.stack(k_s), jnp.stack(v_s),
            jnp.stack(pool_p), jnp.stack(conv_p), jnp.stack(h_p), jnp.stack(pool_s), jnp.stack(conv_s),
            jnp.stack(h_s))
```

```python
import functools

import jax
import jax.numpy as jnp
from jax import lax
from jax.experimental import pallas as pl
from jax.experimental.pallas import tpu as pltpu

F32 = jnp.float32
BF16 = jnp.bfloat16

RMS_EPS = 1e-6
FFN_RESIDUAL = 0.5
POOL_WINDOWS = (2, 4, 8, 16)
POOL_HIST = max(POOL_WINDOWS) - 1
CONV_WIDTH = 4
LRU_C = 8.0
HEAD_DIM = 64

V7X_VMEM_BYTES = 64 * 2**20
V7X_MXU_DIM = 256
LANES = 128
SUBLANES = 8
VMEM_LIMIT = V7X_VMEM_BYTES - 8 * 2**20

TOKEN_TILE_CAP = 768
REC_TILE = 256
ATTN_TILE = V7X_MXU_DIM
ATTN_Q_TILE = 2 * ATTN_TILE
ATTN_HEADS = 8
FFN_CHUNK = 1024
DECODE_PAGES_PER_STEP = 8
POOL_CARRY = 16
CONV_CARRY = 8


def _pick_tile(n, cap):
    best = SUBLANES
    for t in range(SUBLANES, cap + 1, SUBLANES):
        if n % t == 0:
            best = t
    return best


def _params(*sem):
    return pltpu.CompilerParams(dimension_semantics=sem, vmem_limit_bytes=VMEM_LIMIT)


def _const_spec(shape):
    nd = len(shape)
    return pl.BlockSpec(shape, lambda *_: (0,) * nd, pipeline_mode=pl.Buffered(1))


def _rms(x, g):
    ms = jnp.mean(x * x, axis=-1, keepdims=True)
    return x * lax.rsqrt(ms + RMS_EPS) * g


def _dot(a, b):
    return jnp.dot(a, b, preferred_element_type=F32)


def _softplus(z):
    return jnp.maximum(z, 0.0) + jnp.log(1.0 + jnp.exp(-jnp.abs(z)))


LOG2E = 1.4426950408889634


def _softplus_log2(z2):
    neg_abs = lax.bitcast_convert_type(lax.bitcast_convert_type(z2, jnp.uint32) | jnp.uint32(0x80000000), F32)
    return jnp.maximum(z2, 0.0) + jnp.log(1.0 + jnp.exp2(neg_abs)) * LOG2E


def _blockdiag(w, grp):
    h, c, d = w.shape
    w = w.reshape(h // grp, grp, c, d)
    eye = jnp.eye(grp, dtype=w.dtype)
    return jnp.einsum('kgcd,gh->kgchd', w, eye).reshape(h // grp, grp * c, grp * d)


def _joined_rows(main_ref, tail_ref, tail_rows):
    a = main_ref[...]
    keep = a.shape[0] - tail_rows
    mixed = jnp.concatenate([a[:keep], tail_ref[...]], axis=0)
    return jnp.where(pl.program_id(0) == pl.num_programs(0) - 1, mixed, a)


def _ffn_body(*refs, chunks, x_tail, pre_tail, has_pre, has_final):
    it = iter(refs)
    x_ref = next(it)
    xs_ref = next(it) if x_tail else None
    if has_pre:
        o_ref = next(it)
        os_ref = next(it) if pre_tail else None
        wo_ref = next(it)
    g_ref, win_ref, wout_ref = next(it), next(it), next(it)
    if has_final:
        gf_ref = next(it)
    out_ref = next(it)

    x = _joined_rows(x_ref, xs_ref, x_tail) if x_tail else x_ref[...]
    if has_pre:
        o = _joined_rows(o_ref, os_ref, pre_tail) if pre_tail else o_ref[...]
        x = x + _dot(o, wo_ref[...])
    h = _rms(x, g_ref[...]).astype(BF16)
    dff = wout_ref.shape[0]
    acc = None
    for s, n in chunks:
        gate = _dot(h, win_ref[:, s:s + n])
        up = _dot(h, win_ref[:, dff + s:dff + s + n])
        act = (gate * jax.nn.sigmoid(gate) * up).astype(BF16)
        part = _dot(act, wout_ref[s:s + n, :])
        acc = part if acc is None else acc + part
    y = x + FFN_RESIDUAL * acc
    if has_final:
        y = _rms(y, gf_ref[...])
    out_ref[...] = y


def _layer_spec(stacked, layer):
    zeros = (0,) * (stacked.ndim - 1)
    return pl.BlockSpec((None,) + stacked.shape[1:], lambda *_: (layer,) + zeros, pipeline_mode=pl.Buffered(1))


def _ffn(x, g, w_in_all, w_out_all, layer, pre=None, final_g=None):
    def split(a):
        return (a[0], a[1], a[1].shape[0]) if isinstance(a, tuple) else (a, None, 0)

    x_main, x_tail, x_tail_rows = split(x)
    n, d = x_main.shape[0] + x_tail_rows, x_main.shape[1]
    dff = w_out_all.shape[1]
    tm = _pick_tile(n, TOKEN_TILE_CAP)
    chunks = tuple((s, min(FFN_CHUNK, dff - s)) for s in range(0, dff, FFN_CHUNK))
    row = lambda width: pl.BlockSpec((tm, width), lambda i: (i, 0))
    whole = lambda a: pl.BlockSpec(a.shape, lambda i: (0, 0))

    def rows_args(main, tail, tail_rows):
        if tail is None:
            return [main], [row(main.shape[1])]
        assert tail_rows < tm and (main.shape[0] + tail_rows) % tm == 0
        return [main, tail], [row(main.shape[1]), whole(tail)]

    args, specs = rows_args(x_main, x_tail, x_tail_rows)
    pre_tail_rows = 0
    if pre is not None:
        o, wo = pre
        o_main, o_tail, pre_tail_rows = split(o)
        a, s = rows_args(o_main, o_tail, pre_tail_rows)
        args += a + [wo]
        specs += s + [_const_spec(wo.shape)]
    args += [g.reshape(1, d), w_in_all, w_out_all]
    specs += [_const_spec((1, d)), _layer_spec(w_in_all, layer), _layer_spec(w_out_all, layer)]
    if final_g is not None:
        args.append(final_g.reshape(1, d))
        specs.append(_const_spec((1, d)))
    body = functools.partial(_ffn_body, chunks=chunks, x_tail=x_tail_rows, pre_tail=pre_tail_rows,
                             has_pre=pre is not None, has_final=final_g is not None)
    return pl.pallas_call(
        body,
        grid=(n // tm,),
        in_specs=specs,
        out_specs=row(d),
        out_shape=jax.ShapeDtypeStruct((n, d), F32),
        input_output_aliases={} if x_tail is not None else {0: 0},
        compiler_params=_params("arbitrary"),
        name="ffn",
    )(*args)


def _proj_body(x_ref, g_ref, w_ref, *out_refs):
    h = _rms(x_ref[...], g_ref[...]).astype(BF16)
    wd = w_ref.shape[1] // len(out_refs)
    for k, o in enumerate(out_refs):
        o[...] = _dot(h, w_ref[:, k * wd:(k + 1) * wd])


def _proj(x, g, w, n_out, rows=None):
    d = x.shape[1]
    row0, n = (0, x.shape[0]) if rows is None else rows
    wd = w.shape[1] // n_out
    tm = _pick_tile(n, TOKEN_TILE_CAP)
    assert row0 % tm == 0
    blk0 = row0 // tm
    return pl.pallas_call(
        _proj_body,
        grid=(n // tm,),
        in_specs=[pl.BlockSpec((tm, d), lambda i: (blk0 + i, 0)), _const_spec((1, d)), _const_spec(w.shape)],
        out_specs=[pl.BlockSpec((tm, wd), lambda i: (i, 0))] * n_out,
        out_shape=[jax.ShapeDtypeStruct((n, wd), F32)] * n_out,
        compiler_params=_params("arbitrary"),
        name="proj",
    )(x, g.reshape(1, d), w)


def _grouped_dot(x, w_ref):
    blk = w_ref.shape[1]
    xb = x.astype(BF16)
    return jnp.concatenate([_dot(xb[:, k * blk:(k + 1) * blk], w_ref[k]) for k in range(w_ref.shape[0])], axis=-1)


def _log_sigmoid(x):
    return jnp.minimum(x, 0.0) - jnp.log(1.0 + jnp.exp(-jnp.abs(x)))


def _lru_coeffs(conv, wr_ref, br_ref, wi_ref, bi_ref, lam_ref):
    r = jax.nn.sigmoid(_grouped_dot(conv, wr_ref) + br_ref[...])
    i = jax.nn.sigmoid(_grouped_dot(conv, wi_ref) + bi_ref[...])
    log_a = LRU_C * r * _log_sigmoid(lam_ref[...])
    a = jnp.exp(log_a)
    b = jnp.sqrt(1.0 - jnp.exp(2.0 * log_a)) * (i * conv)
    return a, b


def _rec_prompt_body(up_ref, ux_ref, ug_ref, wp_ref, ps_ref, cw_ref, cb_ref, wr_ref, br_ref, wi_ref, bi_ref,
                     lam_ref, mix_ref, pst_ref, cst_ref, hst_ref, pext, cext, a_scr, b_scr, hs_scr, h_scr, *, tm):
    i = pl.program_id(1)
    dp = up_ref.shape[1]
    pg = dp // len(POOL_WINDOWS)

    @pl.when(i == 0)
    def _():
        pext[0:POOL_CARRY, :] = jnp.zeros((POOL_CARRY, dp), F32)
        cext[0:CONV_CARRY, :] = jnp.zeros((CONV_CARRY, cext.shape[1]), F32)
        h_scr[...] = jnp.zeros(h_scr.shape, F32)

    up = up_ref[...]
    pext[POOL_CARRY:POOL_CARRY + tm, :] = up
    pos = i * tm + lax.broadcasted_iota(jnp.int32, (tm, 1), 0)
    pooled = []
    for g, w in enumerate(POOL_WINDOWS):
        cols = slice(g * pg, (g + 1) * pg)
        s = up[:, cols]
        for j in range(1, w):
            s = s + pext[POOL_CARRY - j:POOL_CARRY - j + tm, cols]
        cnt = jnp.minimum(w, pos + 1).astype(F32)
        pooled.append(s / cnt - up[:, cols])
    pool_out = _grouped_dot(jnp.concatenate(pooled, axis=-1), wp_ref) * ps_ref[...]

    ux = ux_ref[...]
    cext[CONV_CARRY:CONV_CARRY + tm, :] = ux
    conv = cb_ref[...] + ux * cw_ref[CONV_WIDTH - 1:CONV_WIDTH, :]
    for j in range(CONV_WIDTH - 1):
        off = CONV_CARRY - (CONV_WIDTH - 1) + j
        conv = conv + cext[off:off + tm, :] * cw_ref[j:j + 1, :]

    a, b = _lru_coeffs(conv, wr_ref, br_ref, wi_ref, bi_ref, lam_ref)
    a_scr[...] = a
    b_scr[...] = b

    def step(t, h):
        h = a_scr[pl.ds(t, 1), :] * h + b_scr[pl.ds(t, 1), :]
        hs_scr[pl.ds(t, 1), :] = h
        return h

    h_last = lax.fori_loop(0, tm, step, h_scr[0:1, :], unroll=8)
    h_scr[...] = jnp.broadcast_to(h_last, h_scr.shape)

    lru_out = hs_scr[...] * jax.nn.gelu(ug_ref[...])
    mix_ref[:, 0:dp] = pool_out.astype(mix_ref.dtype)
    mix_ref[:, dp:] = lru_out.astype(mix_ref.dtype)

    pext[0:POOL_CARRY, :] = pext[tm:tm + POOL_CARRY, :]
    cext[0:CONV_CARRY, :] = cext[tm:tm + CONV_CARRY, :]
    pst_ref[...] = pext[0:POOL_CARRY, :]
    cst_ref[...] = cext[0:CONV_CARRY, :]
    hst_ref[...] = h_scr[...]


def _rec_weights(pool_w, pool_scale, conv_w, conv_b, w_r, b_r, w_i, b_i, lam):
    dp = pool_scale.shape[0]
    dl = conv_b.shape[0]
    wp = _blockdiag(pool_w, V7X_MXU_DIM // pool_w.shape[1]).astype(BF16)
    wr = _blockdiag(w_r, V7X_MXU_DIM // w_r.shape[1]).astype(BF16)
    wi = _blockdiag(w_i, V7X_MXU_DIM // w_i.shape[1]).astype(BF16)
    return (wp, pool_scale.reshape(1, dp), conv_w, conv_b.reshape(1, dl), wr, b_r.reshape(1, dl), wi,
            b_i.reshape(1, dl), lam.reshape(1, dl))


def _rec_prompt(u_pool, u_x, u_gate, weights, nb, t):
    dp, dl = u_pool.shape[1], u_x.shape[1]
    tm = _pick_tile(t, REC_TILE)
    nt = t // tm
    row = lambda width: pl.BlockSpec((tm, width), lambda b, i: (b * nt + i, 0))
    state = lambda rows, width: pl.BlockSpec((None, rows, width), lambda b, i: (b, 0, 0))
    return pl.pallas_call(
        functools.partial(_rec_prompt_body, tm=tm),
        grid=(nb, nt),
        in_specs=[row(dp), row(dl), row(dl)] + [_const_spec(w.shape) for w in weights],
        out_specs=[row(dp + dl), state(POOL_CARRY, dp), state(CONV_CARRY, dl), state(SUBLANES, dl)],
        out_shape=[jax.ShapeDtypeStruct((nb * t, dp + dl), BF16),
                   jax.ShapeDtypeStruct((nb, POOL_CARRY, dp), F32),
                   jax.ShapeDtypeStruct((nb, CONV_CARRY, dl), F32),
                   jax.ShapeDtypeStruct((nb, SUBLANES, dl), F32)],
        scratch_shapes=[pltpu.VMEM((tm + POOL_CARRY, dp), F32), pltpu.VMEM((tm + CONV_CARRY, dl), F32),
                        pltpu.VMEM((tm, dl), F32), pltpu.VMEM((tm, dl), F32), pltpu.VMEM((tm, dl), F32),
                        pltpu.VMEM((SUBLANES, dl), F32)],
        compiler_params=_params("arbitrary", "arbitrary"),
        name="rec_prompt",
    )(u_pool, u_x, u_gate, *weights)


def _rec_sample_body(up_ref, ux_ref, ug_ref, sp_ref, sc_ref, sh_ref, wp_ref, ps_ref, cw_ref, cb_ref, wr_ref,
                     br_ref, wi_ref, bi_ref, lam_ref, mix_ref, np_ref, nc_ref, nh_ref, *, start_pos):
    dp = up_ref.shape[1]
    dl = ux_ref.shape[1]
    pg = dp // len(POOL_WINDOWS)
    up = up_ref[...]
    pooled = []
    for g, w in enumerate(POOL_WINDOWS):
        s = up[:, g * pg:(g + 1) * pg]
        for j in range(1, w):
            base = (POOL_HIST - j) * dp + g * pg
            s = s + sp_ref[:, base:base + pg]
        pooled.append(s / float(min(w, start_pos + 1)) - up[:, g * pg:(g + 1) * pg])
    pool_out = _grouped_dot(jnp.concatenate(pooled, axis=-1), wp_ref) * ps_ref[...]
    np_ref[:, 0:(POOL_HIST - 1) * dp] = sp_ref[:, dp:]
    np_ref[:, (POOL_HIST - 1) * dp:] = up

    ux = ux_ref[...]
    conv = cb_ref[...] + ux * cw_ref[CONV_WIDTH - 1:CONV_WIDTH, :]
    for j in range(CONV_WIDTH - 1):
        conv = conv + sc_ref[:, j * dl:(j + 1) * dl] * cw_ref[j:j + 1, :]
    nc_ref[:, 0:(CONV_WIDTH - 2) * dl] = sc_ref[:, dl:]
    nc_ref[:, (CONV_WIDTH - 2) * dl:] = ux

    a, b = _lru_coeffs(conv, wr_ref, br_ref, wi_ref, bi_ref, lam_ref)
    h = a * sh_ref[...] + b
    nh_ref[...] = h
    mix_ref[:, 0:dp] = pool_out.astype(mix_ref.dtype)
    mix_ref[:, dp:] = (h * jax.nn.gelu(ug_ref[...])).astype(mix_ref.dtype)


def _rec_sample(u_pool, u_x, u_gate, s_pool, s_conv, s_h, weights, row0, start_pos):
    ns = s_h.shape[0]
    dp, dl = u_pool.shape[1], u_x.shape[1]
    blk = row0 // ns
    rows = lambda width: pl.BlockSpec((ns, width), lambda i: (blk, 0))
    full = lambda a: pl.BlockSpec(a.shape, lambda i: (0,) * a.ndim)
    outs = [jax.ShapeDtypeStruct((ns, dp + dl), BF16), jax.ShapeDtypeStruct(s_pool.shape, F32),
            jax.ShapeDtypeStruct(s_conv.shape, F32), jax.ShapeDtypeStruct(s_h.shape, F32)]
    return pl.pallas_call(
        functools.partial(_rec_sample_body, start_pos=start_pos),
        grid=(1,),
        in_specs=[rows(dp), rows(dl), rows(dl), full(s_pool), full(s_conv), full(s_h)] + [full(w) for w in weights],
        out_specs=[pl.BlockSpec(o.shape, lambda i: (0, 0)) for o in outs],
        out_shape=outs,
        compiler_params=_params("arbitrary"),
        name="rec_sample",
    )(u_pool, u_x, u_gate, s_pool, s_conv, s_h, *weights)


def _qkv_prompt_body(*refs, n_prev):
    x_ref, g_ref, wq_ref, wkt_ref, wvt_ref, wv_ref = refs[:6]
    prev = refs[6:6 + 2 * min(n_prev, 1)]
    q_ref, ktf_ref, ktb_ref, vtf_ref, vb_ref = refs[6 + len(prev):]
    nt_dims = (((1,), (1,)), ((), ()))
    h = _rms(x_ref[...], g_ref[...]).astype(BF16)
    q_ref[...] = (_dot(h, wq_ref[...]) * (HEAD_DIM ** -0.5 * LOG2E)).astype(q_ref.dtype)
    kt = lax.dot_general(wkt_ref[...], h, nt_dims, preferred_element_type=F32)
    tile = ktb_ref.shape[2]
    for s in range(ktb_ref.shape[0]):
        ktb_ref[s] = kt[:, s * tile:(s + 1) * tile].astype(ktb_ref.dtype)
    vb_ref[...] = _dot(h, wv_ref[...]).astype(vb_ref.dtype)
    for l in range(n_prev):
        ktf_ref[l] = prev[0][l]
        vtf_ref[l] = prev[1][l]
    ktf_ref[n_prev] = kt
    vtf_ref[n_prev] = lax.dot_general(wvt_ref[...], h, nt_dims, preferred_element_type=F32)


def _qkv_prompt(x, g, w_qkv, nb, t, prev=None):
    d = x.shape[1]
    tm = _pick_tile(t, TOKEN_TILE_CAP)
    tile = ATTN_TILE
    assert tm % tile == 0
    nt = t // tm
    n_prev = 0 if prev is None else prev[0].shape[0]
    wq, wk, wv = (w_qkv[:, k * d:(k + 1) * d] for k in range(3))
    row = pl.BlockSpec((tm, d), lambda b, i: (b * nt + i, 0))
    stack = lambda layers: pl.BlockSpec((layers, None, d, tm), lambda b, i: (0, b, 0, i))
    stack_shape = jax.ShapeDtypeStruct((n_prev + 1, nb, d, t), F32)
    return pl.pallas_call(
        functools.partial(_qkv_prompt_body, n_prev=n_prev),
        grid=(nb, nt),
        in_specs=[row, _const_spec((1, d))] + [_const_spec((d, d))] * 4 + [stack(n_prev)] * (2 if n_prev else 0),
        out_specs=[row, stack(n_prev + 1), pl.BlockSpec((None, tm // tile, d, tile), lambda b, i: (b, i, 0, 0)),
                   stack(n_prev + 1), row],
        out_shape=[jax.ShapeDtypeStruct((nb * t, d), BF16), stack_shape,
                   jax.ShapeDtypeStruct((nb, t // tile, d, tile), BF16), stack_shape,
                   jax.ShapeDtypeStruct((nb * t, d), BF16)],
        compiler_params=_params("arbitrary", "arbitrary"),
        name="qkv_prompt",
    )(x, g.reshape(1, d), wq, wk.T, wv.T, wv, *(prev or ()))


MASKED_LOGIT = -1e30


def _attn_prompt_body(bias_ref, q_ref, kt_ref, v_ref, tri_ref, o_ref, z_scr, sp_scr, e_scr, acc_scr, carry_scr,
                      *, tile, q_tile, heads):
    hg = pl.program_id(1)
    qi = pl.program_id(2)
    pair = 2 * HEAD_DIM
    lane = lax.broadcasted_iota(jnp.int32, (1, pair), 1)
    groups = [slice((h // 2) * pair, (h // 2 + 1) * pair) for h in range(heads)]
    qh = []
    for h in range(heads):
        qg = q_ref[:, groups[h]]
        qh.append(jnp.where((lane // HEAD_DIM) == h % 2, qg, jnp.zeros_like(qg)))
    bias = [bias_ref[heads * hg + h] * LOG2E for h in range(heads)]

    def stage_a(par, j, masked):
        for h in range(heads):
            z = _dot(qh[h], kt_ref[j, groups[h], :]) + bias[h]
            if masked:
                q_pos = qi * q_tile + lax.broadcasted_iota(jnp.int32, (q_tile, tile), 0)
                k_pos = j * tile + lax.broadcasted_iota(jnp.int32, (q_tile, tile), 1)
                z = jnp.where(k_pos < q_pos, z, MASKED_LOGIT)
            z_scr[par, h] = z
            sp_scr[par, h] = _softplus_log2(z).astype(sp_scr.dtype)

    def stage_b(par):
        for h in range(heads):
            local = _dot(sp_scr[par, h], tri_ref[...])
            carry = carry_scr[h]
            e_scr[par, h] = z_scr[par, h] - (local + jnp.concatenate([carry] * (tile // carry.shape[1]), axis=1))
            carry_scr[h] = carry + local[:, 0:1]

    def stage_c(par, j):
        start = pl.multiple_of(j * tile, tile)
        for h in range(heads):
            vblk = v_ref[pl.ds(start, tile), groups[h]]
            w = jnp.exp2(e_scr[par, h])
            acc_scr[h] += _dot(w.astype(vblk.dtype), vblk)

    acc_scr[...] = jnp.zeros(acc_scr.shape, F32)
    carry_scr[...] = jnp.zeros(carry_scr.shape, F32)
    n = 2 * (qi + 1)
    top = n - 1
    stage_a(0, top, True)
    stage_b(0)
    stage_a(1, top - 1, True)

    def two_steps(k, _):
        i = 2 * k
        stage_c(0, top - i + 2)
        stage_b(1)
        stage_a(0, top - i, False)
        stage_c(1, top - i + 1)
        stage_b(0)
        stage_a(1, top - i - 1, False)
        return 0

    lax.fori_loop(1, qi + 1, two_steps, 0)
    stage_c(0, 1)
    stage_b(1)
    stage_c(1, 0)
    for g in range(heads // 2):
        o_ref[:, g * pair:(g + 1) * pair] = jnp.where(lane < HEAD_DIM, acc_scr[2 * g],
                                                      acc_scr[2 * g + 1]).astype(o_ref.dtype)


def _attn_prompt(q, kt_blocks, v, bias, nb, t):
    d = q.shape[1]
    tile, q_tile, heads = ATTN_TILE, ATTN_Q_TILE, ATTN_HEADS
    assert q_tile == 2 * tile and t % q_tile == 0 and heads % 2 == 0
    nq = t // q_tile
    width = heads * HEAD_DIM
    tri = jnp.tril(jnp.ones((tile, tile), BF16))
    stage_buf = lambda dtype: pltpu.VMEM((2, heads, q_tile, tile), dtype)
    row_buf = pltpu.VMEM((heads, q_tile, 2 * HEAD_DIM), F32)
    return pl.pallas_call(
        functools.partial(_attn_prompt_body, tile=tile, q_tile=q_tile, heads=heads),
        grid=(nb, d // width, nq),
        in_specs=[pl.BlockSpec(memory_space=pltpu.SMEM),
                  pl.BlockSpec((q_tile, width), lambda b, hg, qi: (b * nq + qi, hg)),
                  pl.BlockSpec((None, t // tile, width, tile), lambda b, hg, qi: (b, 0, hg, 0),
                               pipeline_mode=pl.Buffered(1)),
                  pl.BlockSpec((t, width), lambda b, hg, qi: (b, hg), pipeline_mode=pl.Buffered(1)),
                  _const_spec((tile, tile))],
        out_specs=pl.BlockSpec((q_tile, width), lambda b, hg, qi: (b * nq + qi, hg)),
        out_shape=jax.ShapeDtypeStruct((nb * t, d), BF16),
        scratch_shapes=[stage_buf(F32), stage_buf(BF16), stage_buf(F32), row_buf, row_buf],
        compiler_params=_params("arbitrary", "arbitrary", "arbitrary"),
        name="attn_prompt",
    )(bias, q, kt_blocks, v, tri)


def _attn_decode_body(pt_ref, q_ref, bias_ref, tri_ref, *refs, group, nh):
    k_refs, v_refs = refs[:group], refs[group:2 * group]
    o_ref, acc_scr, carry_scr = refs[2 * group:]
    n = pl.program_id(0)
    p = pl.program_id(1)
    d, page = acc_scr.shape
    hd = d // nh

    @pl.when((n == 0) & (p == 0))
    def _():
        o_ref[...] = jnp.zeros(o_ref.shape, F32)

    @pl.when(p == 0)
    def _():
        acc_scr[...] = jnp.zeros(acc_scr.shape, F32)
        carry_scr[...] = jnp.zeros(carry_scr.shape, F32)

    q = q_ref[...]
    tri = tri_ref[...]
    for g in range(group):
        z = (k_refs[g][...] * q).reshape(nh, hd, page).sum(axis=1) + bias_ref[...]
        sp = _softplus(z)
        hi = sp.astype(BF16)
        lo = (sp - hi.astype(F32)).astype(BF16)
        sums = _dot(hi, tri) + _dot(lo, tri)
        carry = carry_scr[...]
        w = jnp.exp(z - (sums[:, :page] + carry))
        carry_scr[...] = carry + sums[:, page:]
        v = v_refs[g][...].reshape(nh, hd, page)
        acc_scr[...] += (v * w[:, None, :]).reshape(d, page)

    @pl.when(p == pl.num_programs(1) - 1)
    def _():
        col = jnp.sum(acc_scr[...], axis=1, keepdims=True)
        onehot = lax.broadcasted_iota(jnp.int32, (1, o_ref.shape[1]), 1) == n
        o_ref[...] += jnp.where(onehot, col, 0.0)


def _paged_cache_view(cache):
    nl, nphys, page, nh, hd = cache.shape
    return jnp.transpose(cache, (0, 1, 3, 4, 2)).reshape(nl, nphys, nh * hd, page)


def _attn_decode(q, cache_kt, cache_vt, layer, page_table, bias):
    ns, npages = page_table.shape
    d, page = cache_kt.shape[2:]
    nh = bias.shape[0]
    group = max(g for g in range(1, DECODE_PAGES_PER_STEP + 1) if npages % g == 0)
    q_rep = jnp.broadcast_to((q * (HEAD_DIM ** -0.5))[:, :, None], (ns, d, page))
    bias_rep = jnp.broadcast_to(bias[:, None], (nh, page)).astype(F32)
    tri = jnp.concatenate([jnp.tril(jnp.ones((page, page), BF16)), jnp.ones((page, page), BF16)], axis=1)

    def page_spec(g):
        return pl.BlockSpec((None, None, d, page),
                            lambda n, p, pt: (layer, pt[n * npages + npages - 1 - (p * group + g)], 0, 0))

    const = lambda shape: pl.BlockSpec(shape, lambda n, p, pt: (0,) * len(shape))
    pages = [page_spec(g) for g in range(group)]
    grid_spec = pltpu.PrefetchScalarGridSpec(
        num_scalar_prefetch=1,
        grid=(ns, npages // group),
        in_specs=[pl.BlockSpec((None, d, page), lambda n, p, pt: (n, 0, 0)), const((nh, page)),
                  const((page, 2 * page))] + pages + pages,
        out_specs=pl.BlockSpec((d, ns), lambda n, p, pt: (0, 0)),
        scratch_shapes=[pltpu.VMEM((d, page), F32), pltpu.VMEM((nh, page), F32)],
    )
    out_t = pl.pallas_call(
        functools.partial(_attn_decode_body, group=group, nh=nh),
        grid_spec=grid_spec,
        out_shape=jax.ShapeDtypeStruct((d, ns), F32),
        compiler_params=_params("arbitrary", "arbitrary"),
        name="attn_decode",
    )(page_table.reshape(-1), q_rep, bias_rep, tri, *([cache_kt] * group), *([cache_vt] * group))
    return out_t.T


def kernel(x_prompt, x_sample, cache_k, cache_v, page_table, state_pool, state_conv, state_h, norm_ffn_a, ffn_a_in, ffn_a_out, norm_mix, norm_ffn_b, ffn_b_in, ffn_b_out, rec_w_in, pool_w, pool_scale, conv_w, conv_b, gate_r_w, gate_r_b, gate_i_w, gate_i_b, lru_lambda, rec_w_out, attn_w_qkv, attn_logit_bias, attn_w_out, norm_final):
    nb, t, d = x_prompt.shape
    ns, t_dec, _ = x_sample.shape
    assert t_dec == 1 and (nb * t) % ns == 0 and t % ATTN_TILE == 0
    depth = norm_mix.shape[0]
    nh = d // HEAD_DIM
    past_len = page_table.shape[1] * cache_k.shape[2]
    n_p = nb * t
    dp, dl = pool_scale.shape[1], conv_b.shape[1]

    x = (x_prompt.reshape(n_p, d), x_sample.reshape(ns, d))
    bf = lambda w: w.astype(BF16)
    ffn_a_in_bf, ffn_a_out_bf, ffn_b_in_bf, ffn_b_out_bf = bf(ffn_a_in), bf(ffn_a_out), bf(ffn_b_in), bf(ffn_b_out)

    k_s, v_s = [], []
    pool_p, conv_p, h_p, pool_s, conv_s, h_s = [], [], [], [], [], []
    cache_kt, cache_vt = _paged_cache_view(cache_k), _paged_cache_view(cache_v)
    kv_t = None
    for l in range(depth):
        x = _ffn(x, norm_ffn_a[l], ffn_a_in_bf, ffn_a_out_bf, l)
        if l % 2 == 0:
            r = l // 2
            u_pool, u_x, u_gate = _proj(x, norm_mix[l], bf(rec_w_in[r]), 3)
            weights = _rec_weights(pool_w[r], pool_scale[r], conv_w[r], conv_b[r], gate_r_w[r], gate_r_b[r],
                                   gate_i_w[r], gate_i_b[r], lru_lambda[r])
            mix_p, st_pool, st_conv, st_h = _rec_prompt(u_pool, u_x, u_gate, weights, nb, t)
            mix_s, new_pool, new_conv, new_h = _rec_sample(
                u_pool, u_x, u_gate, state_pool[r].reshape(ns, POOL_HIST * dp),
                state_conv[r].reshape(ns, (CONV_WIDTH - 1) * dl), state_h[r], weights, n_p, past_len)
            pool_p.append(st_pool[:, POOL_CARRY - POOL_HIST:])
            conv_p.append(st_conv[:, CONV_CARRY - (CONV_WIDTH - 1):])
            h_p.append(st_h[:, 0])
            pool_s.append(new_pool.reshape(ns, POOL_HIST, dp))
            conv_s.append(new_conv.reshape(ns, CONV_WIDTH - 1, dl))
            h_s.append(new_h)
            pre = ((mix_p, mix_s), bf(rec_w_out[r]))
        else:
            a = l // 2
            w_qkv = bf(attn_w_qkv[a])
            q_bf, kt_all, kt_blocks, vt_all, v_bf = _qkv_prompt(x, norm_mix[l], w_qkv, nb, t, prev=kv_t)
            kv_t = (kt_all, vt_all)
            q_s, k_new, v_new = _proj(x, norm_mix[l], w_qkv, 3, rows=(n_p, ns))
            o_p = _attn_prompt(q_bf, kt_blocks, v_bf, attn_logit_bias[a], nb, t)
            o_s = _attn_decode(q_s, cache_kt, cache_vt, a, page_table, attn_logit_bias[a])
            k_s.append(k_new.reshape(ns, 1, nh, HEAD_DIM))
            v_s.append(v_new.reshape(ns, 1, nh, HEAD_DIM))
            pre = ((o_p, o_s.astype(BF16)), bf(attn_w_out[a]))
        final = norm_final if l == depth - 1 else None
        x = _ffn(x, norm_ffn_b[l], ffn_b_in_bf, ffn_b_out_bf, l, pre=pre, final_g=final)
    y_prompt = x[:n_p].reshape(nb, t, d)
    y_sample = x[n_p:].reshape(ns, 1, d)
    k_p, v_p = (jnp.transpose(a.reshape(a.shape[0], nb, nh, HEAD_DIM, t), (0, 1, 4, 2, 3)) for a in kv_t)
    return (y_prompt, y_sample, k_p, v_p, jnp.stack(k_s), jnp.stack(v_s),
            jnp.stack(pool_p), jnp.stack(conv_p), jnp.stack(h_p), jnp.stack(pool_s), jnp.stack(conv_s),
            jnp.stack(h_s))
```
